```python
import jax, jax.numpy as jnp
from jax import lax
import numpy as np

D_MODEL = 1024
BATCH = 16
SEQ = 2048
DEPTH = 4

A_HEADS = 4
A_DK = 128
A_DV = 128
A_KD = A_HEADS * A_DK
A_WIDTH = A_HEADS * A_DV
A_CHUNK = 64
B_Q_HEADS = 8
B_KV_HEADS = 2
B_HEAD_DIM = 64
B_WIDTH = B_Q_HEADS * B_HEAD_DIM
B_KV_WIDTH = B_KV_HEADS * B_HEAD_DIM
WINDOW = 128
ROPE_DIM = B_HEAD_DIM // 4
ROPE_THETA = 500000.0
AB_IN_WIDTH = 2 * A_KD + 2 * A_WIDTH + B_WIDTH + 2 * B_KV_WIDTH
AB_OUT_WIDTH = A_WIDTH + B_WIDTH
POOL_WINDOWS = (2, 4, 8, 16)
POOL_GROUPS = 4
POOL_GROUP = D_MODEL // POOL_GROUPS
D_FF = 2816
CONV_WIDTH = 3
N_EVEN = (DEPTH + 1) // 2
N_ODD = DEPTH // 2
ALPHA = (2.0 * DEPTH) ** 0.25
BETA = (8.0 * DEPTH) ** -0.25
LN_EPS = 1e-5
RMS_EPS = 1e-6

kernel_name = "hybrid_hgrn2_swa_pool_convffn_deepnorm"

F32 = jnp.float32


def _layer_norm(x, g, b):
    xf = x.astype(F32)
    mu = xf.mean(-1, keepdims=True)
    var = jnp.mean(jnp.square(xf - mu), -1, keepdims=True)
    return ((xf - mu) * lax.rsqrt(var + LN_EPS) * g + b).astype(x.dtype)


def _rope(x, cos, sin):
    half = ROPE_DIM // 2
    x1 = x[..., :half].astype(F32)
    x2 = x[..., half:ROPE_DIM].astype(F32)
    r1 = x1 * cos - x2 * sin
    r2 = x2 * cos + x1 * sin
    return jnp.concatenate([r1.astype(x.dtype), r2.astype(x.dtype), x[..., ROPE_DIM:]], -1)


def _hgrn2(q, fz, i, lb):
    Bn, T = q.shape[:2]
    fzf = fz.astype(F32)
    f = lb + (1.0 - lb) * jax.nn.sigmoid(fzf)
    logf = jnp.log(f)
    k = (1.0 - lb) * jax.nn.sigmoid(-fzf)
    nc = T // A_CHUNK

    def to_chunks(a):
        a = a.reshape(Bn, nc, A_CHUNK, A_HEADS, a.shape[-1])
        return a.transpose(1, 0, 3, 2, 4)

    qc, kc, vc, gc = (to_chunks(a) for a in (q.astype(F32), k, i.astype(F32), logf))
    causal = jnp.tril(jnp.ones((A_CHUNK, A_CHUNK), bool))[:, :, None]

    def step(S, inp):
        qb, kb, vb, gb = inp
        b = jnp.cumsum(gb, axis=2)
        o_inter = jnp.einsum('bhtk,bhkv->bhtv', qb * jnp.exp(b), S)
        diff = b[:, :, :, None, :] - b[:, :, None, :, :]
        decay = jnp.exp(jnp.where(causal, diff, -jnp.inf))
        att = jnp.einsum('bhtk,bhsk,bhtsk->bhts', qb, kb, decay)
        o = o_inter + jnp.einsum('bhts,bhsv->bhtv', att, vb)
        b_last = b[:, :, -1:, :]
        S = jnp.exp(b_last[:, :, 0])[..., None] * S + jnp.einsum(
            'bhsk,bhsv->bhkv', kb * jnp.exp(b_last - b), vb)
        return S, o

    S0 = jnp.zeros((Bn, A_HEADS, A_DK, A_DV), F32)
    _, o = lax.scan(step, S0, (qc, kc, vc, gc))
    return o.transpose(1, 0, 3, 2, 4).reshape(Bn, T, A_HEADS, A_DV)


def _swa(q, k, v, sinks):
    Bn, T = q.shape[:2]
    nb = T // WINDOW
    G = B_Q_HEADS // B_KV_HEADS
    qb = q.reshape(Bn, nb, WINDOW, B_KV_HEADS, G, B_HEAD_DIM)

    def band(a):
        a = a.reshape(Bn, nb, WINDOW, B_KV_HEADS, B_HEAD_DIM)
        prev = jnp.pad(a, ((0, 0), (1, 0), (0, 0), (0, 0), (0, 0)))[:, :-1]
        return jnp.concatenate([prev, a], axis=2)

    kb, vb = band(k), band(v)
    s = jnp.einsum('bnqhgd,bnkhd->bnhgqk', qb, kb).astype(F32) * (B_HEAD_DIM ** -0.5)
    qi = jnp.arange(WINDOW)[:, None] + WINDOW
    ki = jnp.arange(2 * WINDOW)[None, :]
    rel = qi - ki
    allowed = (rel >= 0) & (rel < WINDOW)
    blk = jnp.arange(nb)[:, None, None]
    valid = allowed[None] & ((blk > 0) | (ki[None] >= WINDOW))
    s = jnp.where(valid[None, :, None, None], s, -jnp.inf)
    sink = sinks.astype(F32).reshape(B_KV_HEADS, G)[None, None, :, :, None, None]
    m = jnp.maximum(s.max(-1, keepdims=True), sink)
    p = jnp.exp(s - m)
    p = p / (p.sum(-1, keepdims=True) + jnp.exp(sink - m))
    o = jnp.einsum('bnhgqk,bnkhd->bnqhgd', p.astype(v.dtype), vb)
    return o.reshape(Bn, T, B_WIDTH)


def _even_mixer(h, w_in, w_out, lb, norm_g, sinks, cos, sin):
    Bn, T, _ = h.shape
    z = h @ w_in
    widths = [A_KD, A_KD, A_WIDTH, A_WIDTH, B_WIDTH, B_KV_WIDTH, B_KV_WIDTH]
    idx = np.cumsum(widths)[:-1].tolist()
    aq, af, ai, ag, bq, bk, bv = jnp.split(z, idx, axis=-1)
    oa = _hgrn2(aq.reshape(Bn, T, A_HEADS, A_DK), af.reshape(Bn, T, A_HEADS, A_DK),
                ai.reshape(Bn, T, A_HEADS, A_DV), lb.reshape(A_HEADS, A_DK))
    oa = oa * lax.rsqrt(jnp.mean(oa * oa, -1, keepdims=True) + RMS_EPS) * norm_g
    oa = oa.reshape(Bn, T, A_WIDTH).astype(h.dtype) * jax.nn.silu(ag)
    q = _rope(bq.reshape(Bn, T, B_Q_HEADS, B_HEAD_DIM), cos, sin)
    k = _rope(bk.reshape(Bn, T, B_KV_HEADS, B_HEAD_DIM), cos, sin)
    v = bv.reshape(Bn, T, B_KV_HEADS, B_HEAD_DIM)
    ob = _swa(q, k, v, sinks).astype(h.dtype)
    return jnp.concatenate([oa, ob], axis=-1) @ w_out


def _pool_mixer(h, w_grp, scale):
    Bn, T, D = h.shape
    hf = h.astype(F32).reshape(Bn, T, POOL_GROUPS, POOL_GROUP)
    t = jnp.arange(T)
    outs = []
    for gi, w in enumerate(POOL_WINDOWS):
        xg = hf[:, :, gi]
        cs = jnp.cumsum(jnp.pad(xg, ((0, 0), (w, 0), (0, 0))), axis=1)
        win_sum = cs[:, w:] - cs[:, :T]
        cnt = jnp.minimum(t + 1, w).astype(F32)[None, :, None]
        outs.append(win_sum / cnt - xg)
    pooled = jnp.stack(outs, axis=2).astype(h.dtype)
    y = jnp.einsum('btgc,gcd->btgd', pooled, w_grp)
    return y.reshape(Bn, T, D) * scale


def _conv_ffn(h, w_up, conv_w, conv_b, w_down):
    T = h.shape[1]
    u, v = jnp.split(h @ w_up, 2, axis=-1)
    up = jnp.pad(u, ((0, 0), (CONV_WIDTH - 1, 0), (0, 0)))
    uc = conv_b
    for j in range(CONV_WIDTH):
        uc = uc + up[:, j:j + T] * conv_w[j]
    return (jax.nn.silu(uc) * v) @ w_down


def setup_inputs(seed: int = 0) -> dict:
    key = jax.random.key(seed)
    ks = jax.random.split(key, 20)
    n = jax.random.normal
    D = D_MODEL
    x = n(ks[0], (BATCH, SEQ, D), F32)
    c = n(ks[1], (BATCH, D), F32)
    offs = jax.random.randint(ks[2], (BATCH, 1), 0, 4096, dtype=jnp.int32)
    positions = offs + jnp.arange(SEQ, dtype=jnp.int32)[None, :]
    ada_w = n(ks[3], (DEPTH, D, 6 * D), F32) * D ** -0.5
    ada_b = 0.02 * n(ks[4], (DEPTH, 6 * D), F32)
    ln_g = 1.0 + 0.02 * n(ks[5], (DEPTH, 2, D), F32)
    ln_b = 0.02 * n(ks[6], (DEPTH, 2, D), F32)
    ab_w_in = n(ks[7], (N_EVEN, D, AB_IN_WIDTH), F32) * D ** -0.5
    ab_w_out = n(ks[8], (N_EVEN, AB_OUT_WIDTH, D), F32) * (AB_OUT_WIDTH ** -0.5 * BETA)
    hgrn_lb_raw = 0.5 * n(ks[9], (N_EVEN, A_KD), F32)
    hgrn_norm_g = 1.0 + 0.02 * n(ks[10], (N_EVEN, A_DV), F32)
    attn_sinks = 0.5 * n(ks[11], (N_EVEN, B_Q_HEADS), F32)
    pool_w = n(ks[12], (N_ODD, POOL_GROUPS, POOL_GROUP, POOL_GROUP), F32) * (POOL_GROUP ** -0.5 * BETA)
    pool_scale = 1.0 + 0.02 * n(ks[13], (N_ODD, D), F32)
    ffn_w_up = n(ks[14], (DEPTH, D, 2 * D_FF), F32) * D ** -0.5
    ffn_conv_w = n(ks[15], (DEPTH, CONV_WIDTH, D_FF), F32) * CONV_WIDTH ** -0.5
    ffn_conv_b = 0.02 * n(ks[16], (DEPTH, D_FF), F32)
    ffn_w_down = n(ks[17], (DEPTH, D_FF, D), F32) * (D_FF ** -0.5 * BETA)
    return {"x": x, "c": c, "positions": positions, "ada_w": ada_w, "ada_b": ada_b,
            "ln_g": ln_g, "ln_b": ln_b, "ab_w_in": ab_w_in, "ab_w_out": ab_w_out,
            "hgrn_lb_raw": hgrn_lb_raw, "hgrn_norm_g": hgrn_norm_g, "attn_sinks": attn_sinks,
            "pool_w": pool_w, "pool_scale": pool_scale, "ffn_w_up": ffn_w_up,
            "ffn_conv_w": ffn_conv_w, "ffn_conv_b": ffn_conv_b, "ffn_w_down": ffn_w_down}


def reference(x, c, positions, ada_w, ada_b, ln_g, ln_b, ab_w_in, ab_w_out,
              hgrn_lb_raw, hgrn_norm_g, attn_sinks, pool_w, pool_scale,
              ffn_w_up, ffn_conv_w, ffn_conv_b, ffn_w_down):
    lb_p = jax.nn.softmax(hgrn_lb_raw.astype(F32), axis=0)
    lower_bounds = jnp.cumsum(lb_p, axis=0) - lb_p[0]
    inv_freq = ROPE_THETA ** (-jnp.arange(0, ROPE_DIM, 2, dtype=F32) / ROPE_DIM)
    ang = positions.astype(F32)[..., None] * inv_freq
    cos = jnp.cos(ang)[:, :, None, :]
    sin = jnp.sin(ang)[:, :, None, :]
    c_act = jax.nn.silu(c)
    for l in range(DEPTH):
        mod = c_act @ ada_w[l] + ada_b[l]
        sh1, sc1, g1, sh2, sc2, g2 = [m[:, None, :] for m in jnp.split(mod, 6, axis=-1)]
        h = x * (1.0 + sc1) + sh1
        if l % 2 == 0:
            e = l // 2
            y = _even_mixer(h, ab_w_in[e], ab_w_out[e], lower_bounds[e], hgrn_norm_g[e],
                            attn_sinks[e], cos, sin)
        else:
            o = l // 2
            y = _pool_mixer(h, pool_w[o], pool_scale[o])
        x = _layer_norm(ALPHA * x + g1 * y, ln_g[l, 0], ln_b[l, 0])
        h = x * (1.0 + sc2) + sh2
        y = _conv_ffn(h, ffn_w_up[l], ffn_conv_w[l], ffn_conv_b[l], ffn_w_down[l])
        x = _layer_norm(ALPHA * x + g2 * y, ln_g[l, 1], ln_b[l, 1])
    return x
```

```python
import functools

import numpy as np
import jax
import jax.numpy as jnp
from jax import lax
from jax.experimental import pallas as pl
from jax.experimental.pallas import tpu as pltpu

F32 = jnp.float32
BF16 = jnp.bfloat16

A_HEADS = 4
A_DK = 128
A_DV = 128
A_KD = A_HEADS * A_DK
A_WIDTH = A_HEADS * A_DV
A_CHUNK = 64
B_Q_HEADS = 8
B_KV_HEADS = 2
B_HEAD_DIM = 64
B_WIDTH = B_Q_HEADS * B_HEAD_DIM
B_KV_WIDTH = B_KV_HEADS * B_HEAD_DIM
WINDOW = 128
ROPE_DIM = B_HEAD_DIM // 4
ROPE_THETA = 500000.0
POOL_WINDOWS = (2, 4, 8, 16)
POOL_HALO = 16
CONV_WIDTH = 3
LN_EPS = 1e-5
RMS_EPS = 1e-6

LANES = 128
SUBLANES = 8
VMEM_LIMIT_BYTES = 56 * 1024 * 1024

NEG_BIG = -1e30


def _cparams(n_axes):
    return pltpu.CompilerParams(
        dimension_semantics=("arbitrary",) * n_axes,
        vmem_limit_bytes=VMEM_LIMIT_BYTES,
    )


def _dot(a, b):
    return jnp.dot(a, b, preferred_element_type=F32)


def _dot_nt(a, b):
    return lax.dot_general(a, b, (((1,), (1,)), ((), ())), preferred_element_type=F32)


def _dot_tn(a, b):
    return lax.dot_general(a, b, (((0,), (0,)), ((), ())), preferred_element_type=F32)


def _layer_norm(r, g, b):
    mu = jnp.mean(r, axis=-1, keepdims=True)
    d = r - mu
    var = jnp.mean(d * d, axis=-1, keepdims=True)
    return d * lax.rsqrt(var + LN_EPS) * g + b


def _sigmoid_pair(z):
    e = jnp.exp(-jnp.abs(z))
    r = 1.0 / (1.0 + e)
    er = e * r
    pos = z >= 0
    return jnp.where(pos, r, er), jnp.where(pos, er, r)


def _silu(z):
    s, _ = _sigmoid_pair(z)
    return z * s


def _ada_kernel(c_ref, w_ref, b_ref, o_ref):
    c = c_ref[...]
    ca = _silu(c).astype(BF16)
    o_ref[0] = _dot(ca, w_ref[0].astype(BF16)) + b_ref[0]


def _ada_mod(c, ada_w, ada_b, tn=1536):
    depth, d, n = ada_w.shape
    bsz = c.shape[0]
    assert n % tn == 0
    return pl.pallas_call(
        _ada_kernel,
        grid=(depth, n // tn),
        in_specs=[
            pl.BlockSpec((bsz, d), lambda l, j: (0, 0)),
            pl.BlockSpec((1, d, tn), lambda l, j: (l, 0, j)),
            pl.BlockSpec((1, 1, tn), lambda l, j: (l, 0, j)),
        ],
        out_specs=pl.BlockSpec((1, bsz, tn), lambda l, j: (l, 0, j)),
        out_shape=jax.ShapeDtypeStruct((depth, bsz, n), F32),
        compiler_params=_cparams(2),
        name="ada_mod",
    )(c, ada_w, ada_b.reshape(depth, 1, n))


def _inproj_kernel(x_ref, mod_ref, w_ref, za_ref, zb_ref, *, na):
    x = x_ref[0]
    mod = mod_ref[0]
    h = (x * (1.0 + mod[1:2]) + mod[0:1]).astype(BF16)
    z = _dot(h, w_ref[...])
    za_ref[0] = z[:, :na]
    zb_ref[0] = z[:, na:]


def _inproj(x, mod, w_in, tm):
    bsz, t, d = x.shape
    n = w_in.shape[1]
    na = 2 * A_KD + 2 * A_WIDTH
    nb = n - na
    return pl.pallas_call(
        functools.partial(_inproj_kernel, na=na),
        grid=(bsz, t // tm),
        in_specs=[
            pl.BlockSpec((1, tm, d), lambda b, i: (b, i, 0)),
            pl.BlockSpec((1, 6, d), lambda b, i: (b, 0, 0)),
            pl.BlockSpec((d, n), lambda b, i: (0, 0)),
        ],
        out_specs=[
            pl.BlockSpec((1, tm, na), lambda b, i: (b, i, 0)),
            pl.BlockSpec((1, tm, nb), lambda b, i: (b, i, 0)),
        ],
        out_shape=[
            jax.ShapeDtypeStruct((bsz, t, na), F32),
            jax.ShapeDtypeStruct((bsz, t, nb), F32),
        ],
        compiler_params=_cparams(2),
        name="even_inproj",
    )(x, mod, w_in)


def _hgrn_kernel(za_ref, lbraw_ref, ng_ref, o_ref, st_ref, *, layer_e, tq):
    ti = pl.program_id(1)

    @pl.when(ti == 0)
    def _():
        st_ref[...] = jnp.zeros_like(st_ref)

    lbraw = lbraw_ref[...]
    lmax = jnp.max(lbraw, axis=0, keepdims=True)
    lexp = jnp.exp(lbraw - lmax)
    lbp = lexp / jnp.sum(lexp, axis=0, keepdims=True)
    lb_all = jnp.zeros((1, A_KD), F32)
    for i in range(1, layer_e + 1):
        lb_all = lb_all + lbp[i:i + 1]

    c = A_CHUNK
    rows_t = lax.broadcasted_iota(jnp.int32, (tq, A_DK), 0)
    rc_t = rows_t & (c - 1)
    rows = lax.broadcasted_iota(jnp.int32, (c, A_DK), 0)
    sub = rows & (SUBLANES - 1)
    ri = lax.broadcasted_iota(jnp.int32, (c, c), 0)
    ci = lax.broadcasted_iota(jnp.int32, (c, c), 1)
    ng = ng_ref[...]

    for hd in range(A_HEADS):
        lo = hd * A_DK
        q_t = za_ref[0, :, lo:lo + A_DK]
        fz_t = za_ref[0, :, A_KD + lo:A_KD + lo + A_DK]
        v_t = za_ref[0, :, 2 * A_KD + hd * A_DV:2 * A_KD + (hd + 1) * A_DV]
        ag_t = za_ref[0, :, 2 * A_KD + A_WIDTH + hd * A_DV:2 * A_KD + A_WIDTH + (hd + 1) * A_DV]
        lb = lb_all[:, lo:lo + A_DK]
        sp, sn = _sigmoid_pair(fz_t)
        logf = jnp.log(lb + (1.0 - lb) * sp)
        k_t = (1.0 - lb) * sn
        b_t = logf
        s = 1
        while s < c:
            b_t = b_t + jnp.where(rc_t >= s, pltpu.roll(b_t, s, 0), 0.0)
            s *= 2
        gate_t = _silu(ag_t)

        for ch in range(tq // c):
            r0 = ch * c
            qc = q_t[r0:r0 + c]
            kc = k_t[r0:r0 + c]
            vc = v_t[r0:r0 + c]
            bc = b_t[r0:r0 + c]
            b_last = bc[c - 1:c]
            st = st_ref[hd]
            vb = vc.astype(BF16)

            o = _dot_nt((qc * jnp.exp(bc)).astype(BF16), st.astype(BF16))

            att = jnp.zeros((c, c), F32)
            m = c // 2
            while m >= SUBLANES:
                bref = jnp.concatenate(
                    [jnp.broadcast_to(bc[2 * m * j + m - 1:2 * m * j + m], (2 * m, A_DK))
                     for j in range(c // (2 * m))], axis=0)
                upper = (rows & m) != 0
                qm = jnp.where(upper, qc * jnp.exp(jnp.minimum(bc - bref, 0.0)), 0.0)
                km = jnp.where(upper, 0.0, kc * jnp.exp(jnp.minimum(bref - bc, 0.0)))
                a = _dot_nt(qm.astype(BF16), km.astype(BF16))
                att = att + jnp.where((ri ^ ci) < 2 * m, a, 0.0)
                m //= 2
            o = o + _dot(att.astype(BF16), vb)

            for dd in range(SUBLANES):
                if dd == 0:
                    pd = qc * kc
                    vs = vc
                else:
                    ks = pltpu.roll(kc, dd, 0)
                    bs = pltpu.roll(bc, dd, 0)
                    vs = pltpu.roll(vc, dd, 0)
                    pd = qc * ks * jnp.exp(jnp.minimum(bc - bs, 0.0))
                    pd = jnp.where(sub >= dd, pd, 0.0)
                o = o + jnp.sum(pd, axis=-1, keepdims=True) * vs

            ke = (kc * jnp.exp(b_last - bc)).astype(BF16)
            st_ref[hd] = jnp.exp(b_last) * st + _dot_tn(vb, ke)

            o = o * lax.rsqrt(jnp.mean(o * o, axis=-1, keepdims=True) + RMS_EPS) * ng
            o_ref[0, r0:r0 + c, hd * A_DV:(hd + 1) * A_DV] = o * gate_t[r0:r0 + c]


def _hgrn(za, lb_raw, norm_g, layer_e, tq):
    bsz, t, na = za.shape
    return pl.pallas_call(
        functools.partial(_hgrn_kernel, layer_e=layer_e, tq=tq),
        grid=(bsz, t // tq),
        in_specs=[
            pl.BlockSpec((1, tq, na), lambda b, i: (b, i, 0)),
            pl.BlockSpec(lb_raw.shape, lambda b, i: (0, 0)),
            pl.BlockSpec((1, A_DV), lambda b, i: (0, 0)),
        ],
        out_specs=pl.BlockSpec((1, tq, A_WIDTH), lambda b, i: (b, i, 0)),
        out_shape=jax.ShapeDtypeStruct((bsz, t, A_WIDTH), F32),
        scratch_shapes=[pltpu.VMEM((A_HEADS, A_DV, A_DK), F32)],
        compiler_params=_cparams(2),
        name="even_hgrn",
    )(za, lb_raw, norm_g.reshape(1, A_DV))


def _rope_lane_table():
    inv = ROPE_THETA ** (-np.arange(0, ROPE_DIM, 2, dtype=np.float64) / ROPE_DIM)
    half = ROPE_DIM // 2
    tab = np.zeros((1, LANES), np.float32)
    for l in range(LANES):
        dpos = l % B_HEAD_DIM
        if dpos < ROPE_DIM:
            tab[0, l] = inv[dpos % half]
    return tab


def _swa_kernel(zb_ref, pos_ref, invf_ref, sink_ref, o_ref, kp_ref, vp_ref):
    ti = pl.program_id(1)
    w = WINDOW
    half = ROPE_DIM // 2
    g = B_Q_HEADS // B_KV_HEADS

    @pl.when(ti == 0)
    def _():
        kp_ref[...] = jnp.zeros_like(kp_ref)
        vp_ref[...] = jnp.zeros_like(vp_ref)

    ang = pos_ref[0] * invf_ref[...]
    cosv = jnp.cos(ang)
    sinv = jnp.sin(ang)
    dpos = lax.broadcasted_iota(jnp.int32, (w, LANES), 1) & (B_HEAD_DIM - 1)
    s_from_lo = jnp.where((dpos >= half) & (dpos < ROPE_DIM), sinv, 0.0)
    s_from_hi = jnp.where(dpos < half, -sinv, 0.0)

    def rope(xx):
        return (xx * cosv + pltpu.roll(xx, half, 1) * s_from_lo
                + pltpu.roll(xx, LANES - half, 1) * s_from_hi)

    k_new = rope(zb_ref[0, :, B_WIDTH:B_WIDTH + B_KV_WIDTH])
    v_new = zb_ref[0, :, B_WIDTH + B_KV_WIDTH:B_WIDTH + 2 * B_KV_WIDTH]
    kb = jnp.concatenate([kp_ref[...], k_new], axis=0)
    vb = jnp.concatenate([vp_ref[...], v_new], axis=0)
    kp_ref[...] = k_new
    vp_ref[...] = v_new

    qs = [rope(zb_ref[0, :, j * LANES:(j + 1) * LANES]) * (B_HEAD_DIM ** -0.5)
          for j in range(B_WIDTH // LANES)]

    qi = (lax.broadcasted_iota(jnp.int32, (g * w, 2 * w), 0) & (w - 1)) + w
    ki = lax.broadcasted_iota(jnp.int32, (g * w, 2 * w), 1)
    rel = qi - ki
    valid = (rel >= 0) & (rel < w) & ((ti > 0) | (ki >= w))
    grp = lax.broadcasted_iota(jnp.int32, (g * w, 1), 0) >> (w.bit_length() - 1)

    for hk in range(B_KV_HEADS):
        kh = kb[:, hk * B_HEAD_DIM:(hk + 1) * B_HEAD_DIM].astype(BF16)
        vh = vb[:, hk * B_HEAD_DIM:(hk + 1) * B_HEAD_DIM].astype(BF16)
        heads = []
        sink = jnp.zeros((g * w, 1), F32)
        for gi in range(g):
            hq = hk * g + gi
            col = (hq * B_HEAD_DIM) // LANES
            off = (hq * B_HEAD_DIM) % LANES
            heads.append(qs[col][:, off:off + B_HEAD_DIM])
            sink = jnp.where(grp == gi, sink_ref[hq], sink)
        q4 = jnp.concatenate(heads, axis=0).astype(BF16)
        s = _dot_nt(q4, kh)
        s = jnp.where(valid, s, NEG_BIG)
        mx = jnp.maximum(jnp.max(s, axis=-1, keepdims=True), sink)
        p = jnp.exp(s - mx)
        den = jnp.sum(p, axis=-1, keepdims=True) + jnp.exp(sink - mx)
        p = p / den
        o4 = _dot(p.astype(BF16), vh)
        for gi in range(g):
            hq = hk * g + gi
            o_ref[0, :, hq * B_HEAD_DIM:(hq + 1) * B_HEAD_DIM] = o4[gi * w:(gi + 1) * w]


def _swa(zb, pos_b, sinks):
    bsz, t, nb = zb.shape
    w = WINDOW
    return pl.pallas_call(
        _swa_kernel,
        grid=(bsz, t // w),
        in_specs=[
            pl.BlockSpec((1, w, nb), lambda b, i: (b, i, 0)),
            pl.BlockSpec((1, w, LANES), lambda b, i: (b, i, 0)),
            pl.BlockSpec((1, LANES), lambda b, i: (0, 0)),
            pl.BlockSpec(memory_space=pltpu.SMEM),
        ],
        out_specs=pl.BlockSpec((1, w, B_WIDTH), lambda b, i: (b, i, 0)),
        out_shape=jax.ShapeDtypeStruct((bsz, t, B_WIDTH), F32),
        scratch_shapes=[pltpu.VMEM((w, B_KV_WIDTH), F32), pltpu.VMEM((w, B_KV_WIDTH), F32)],
        compiler_params=_cparams(2),
        name="even_swa",
    )(zb, pos_b, jnp.asarray(_rope_lane_table()), sinks)


def _outproj_kernel(x_ref, mod_ref, oa_ref, ob_ref, w_ref, lng_ref, lnb_ref, o_ref, *, alpha):
    x = x_ref[0]
    mod = mod_ref[0]
    wa = oa_ref.shape[-1]
    y = _dot(oa_ref[0].astype(BF16), w_ref[:wa]) + _dot(ob_ref[0].astype(BF16), w_ref[wa:])
    o_ref[0] = _layer_norm(alpha * x + mod[2:3] * y, lng_ref[...], lnb_ref[...])


def _outproj(x, mod, oa, ob, w_out, ln_g, ln_b, alpha, tm):
    bsz, t, d = x.shape
    wa, wb = oa.shape[-1], ob.shape[-1]
    return pl.pallas_call(
        functools.partial(_outproj_kernel, alpha=alpha),
        grid=(bsz, t // tm),
        in_specs=[
            pl.BlockSpec((1, tm, d), lambda b, i: (b, i, 0)),
            pl.BlockSpec((1, 6, d), lambda b, i: (b, 0, 0)),
            pl.BlockSpec((1, tm, wa), lambda b, i: (b, i, 0)),
            pl.BlockSpec((1, tm, wb), lambda b, i: (b, i, 0)),
            pl.BlockSpec((wa + wb, d), lambda b, i: (0, 0)),
            pl.BlockSpec((1, d), lambda b, i: (0, 0)),
            pl.BlockSpec((1, d), lambda b, i: (0, 0)),
        ],
        out_specs=pl.BlockSpec((1, tm, d), lambda b, i: (b, i, 0)),
        out_shape=jax.ShapeDtypeStruct((bsz, t, d), F32),
        compiler_params=_cparams(2),
        name="even_outproj_ln",
    )(x, mod, oa, ob, w_out, ln_g.reshape(1, d), ln_b.reshape(1, d))


def _pool_kernel(x_ref, mod_ref, w_ref, scale_ref, lng_ref, lnb_ref, o_ref, carry_ref, y_ref,
                 *, alpha, tm):
    ti = pl.program_id(1)

    @pl.when(ti == 0)
    def _():
        carry_ref[...] = jnp.zeros_like(carry_ref)

    x = x_ref[0]
    mod = mod_ref[0]
    h = x * (1.0 + mod[1:2]) + mod[0:1]
    ext = jnp.concatenate([carry_ref[...], h], axis=0)
    carry_ref[...] = h[tm - POOL_HALO:]

    t_glob = ti * tm + lax.broadcasted_iota(jnp.int32, (tm, 1), 0)
    gw = x.shape[-1] // len(POOL_WINDOWS)
    for gi, win in enumerate(POOL_WINDOWS):
        e = ext[:, gi * gw:(gi + 1) * gw]
        s = 1
        while s < win:
            e = e + pltpu.roll(e, s, 0)
            s *= 2
        cnt = jnp.minimum(t_glob + 1, win).astype(F32)
        pooled = e[POOL_HALO:] / cnt - h[:, gi * gw:(gi + 1) * gw]
        y_ref[:, gi * gw:(gi + 1) * gw] = _dot(pooled.astype(BF16), w_ref[gi])
    y = y_ref[...] * scale_ref[...]
    o_ref[0] = _layer_norm(alpha * x + mod[2:3] * y, lng_ref[...], lnb_ref[...])


def _pool_mixer(x, mod, w_grp, scale, ln_g, ln_b, alpha, tm):
    bsz, t, d = x.shape
    ng, gw, _ = w_grp.shape
    return pl.pallas_call(
        functools.partial(_pool_kernel, alpha=alpha, tm=tm),
        grid=(bsz, t // tm),
        in_specs=[
            pl.BlockSpec((1, tm, d), lambda b, i: (b, i, 0)),
            pl.BlockSpec((1, 6, d), lambda b, i: (b, 0, 0)),
            pl.BlockSpec((ng, gw, gw), lambda b, i: (0, 0, 0)),
            pl.BlockSpec((1, d), lambda b, i: (0, 0)),
            pl.BlockSpec((1, d), lambda b, i: (0, 0)),
            pl.BlockSpec((1, d), lambda b, i: (0, 0)),
        ],
        out_specs=pl.BlockSpec((1, tm, d), lambda b, i: (b, i, 0)),
        out_shape=jax.ShapeDtypeStruct((bsz, t, d), F32),
        scratch_shapes=[pltpu.VMEM((POOL_HALO, d), F32), pltpu.VMEM((tm, d), F32)],
        compiler_params=_cparams(2),
        name="odd_pool_ln",
    )(x, mod, w_grp, scale.reshape(1, d), ln_g.reshape(1, d), ln_b.reshape(1, d))


def _ffn_kernel(x_ref, mod_ref, wup_ref, cw_ref, cb_ref, wdn_ref, lng_ref, lnb_ref, o_ref,
                tail_ref, acc_ref, *, alpha, tm, fc, dff):
    ti = pl.program_id(1)

    @pl.when(ti == 0)
    def _():
        tail_ref[...] = jnp.zeros_like(tail_ref)

    x = x_ref[0]
    mod = mod_ref[0]
    h = (x * (1.0 + mod[4:5]) + mod[3:4]).astype(BF16)
    row = lax.broadcasted_iota(jnp.int32, (tm, fc), 0)
    for ci in range(dff // fc):
        lo = ci * fc
        u = _dot(h, wup_ref[:, lo:lo + fc])
        v = _dot(h, wup_ref[:, dff + lo:dff + lo + fc])
        tail = tail_ref[:, lo:lo + fc]
        t1 = tail[SUBLANES - 1:SUBLANES]
        t2 = tail[SUBLANES - 2:SUBLANES - 1]
        u1 = jnp.where(row == 0, t1, pltpu.roll(u, 1, 0))
        u2 = jnp.where(row == 0, t2, jnp.where(row == 1, t1, pltpu.roll(u, 2, 0)))
        tail_ref[:, lo:lo + fc] = u[tm - SUBLANES:]
        cw = cw_ref[:, lo:lo + fc]
        uc = cb_ref[:, lo:lo + fc] + u2 * cw[0:1] + u1 * cw[1:2] + u * cw[2:3]
        gact = (_silu(uc) * v).astype(BF16)
        part = _dot(gact, wdn_ref[lo:lo + fc, :])
        if ci == 0:
            acc_ref[...] = part
        else:
            acc_ref[...] += part
    o_ref[0] = _layer_norm(alpha * x + mod[5:6] * acc_ref[...], lng_ref[...], lnb_ref[...])


def _conv_ffn(x, mod, w_up, conv_w, conv_b, w_down, ln_g, ln_b, alpha, tm, fc=256):
    bsz, t, d = x.shape
    dff = w_down.shape[0]
    assert dff % fc == 0 and CONV_WIDTH == 3
    return pl.pallas_call(
        functools.partial(_ffn_kernel, alpha=alpha, tm=tm, fc=fc, dff=dff),
        grid=(bsz, t // tm),
        in_specs=[
            pl.BlockSpec((1, tm, d), lambda b, i: (b, i, 0)),
            pl.BlockSpec((1, 6, d), lambda b, i: (b, 0, 0)),
            pl.BlockSpec((d, 2 * dff), lambda b, i: (0, 0)),
            pl.BlockSpec((CONV_WIDTH, dff), lambda b, i: (0, 0)),
            pl.BlockSpec((1, dff), lambda b, i: (0, 0)),
            pl.BlockSpec((dff, d), lambda b, i: (0, 0)),
            pl.BlockSpec((1, d), lambda b, i: (0, 0)),
            pl.BlockSpec((1, d), lambda b, i: (0, 0)),
        ],
        out_specs=pl.BlockSpec((1, tm, d), lambda b, i: (b, i, 0)),
        out_shape=jax.ShapeDtypeStruct((bsz, t, d), F32),
        scratch_shapes=[pltpu.VMEM((SUBLANES, dff), F32), pltpu.VMEM((tm, d), F32)],
        compiler_params=_cparams(2),
        name="ffn_ln",
    )(x, mod, w_up, conv_w, conv_b.reshape(1, dff), w_down, ln_g.reshape(1, d), ln_b.reshape(1, d))


def _pick_tile(t, pref):
    tm = min(t, pref)
    assert t % tm == 0
    return tm


def kernel(x, c, positions, ada_w, ada_b, ln_g, ln_b, ab_w_in, ab_w_out, hgrn_lb_raw, hgrn_norm_g,
           attn_sinks, pool_w, pool_scale, ffn_w_up, ffn_conv_w, ffn_conv_b, ffn_w_down):
    bsz, t, d = x.shape
    depth = ada_w.shape[0]
    alpha = (2.0 * depth) ** 0.25
    tm = _pick_tile(t, 512)
    tq = _pick_tile(t, 256)

    mod_all = _ada_mod(c, ada_w, ada_b).reshape(depth, bsz, 6, d)
    pos_b = jnp.broadcast_to(positions.astype(F32)[:, :, None], (bsz, t, LANES))
    w_in = ab_w_in.astype(BF16)
    w_out = ab_w_out.astype(BF16)
    w_pool = pool_w.astype(BF16)
    w_up = ffn_w_up.astype(BF16)
    w_down = ffn_w_down.astype(BF16)

    for l in range(depth):
        mod = mod_all[l]
        if l % 2 == 0:
            e = l // 2
            za, zb = _inproj(x, mod, w_in[e], tm)
            oa = _hgrn(za, hgrn_lb_raw, hgrn_norm_g[e], e, tq)
            ob = _swa(zb, pos_b, attn_sinks[e])
            x = _outproj(x, mod, oa, ob, w_out[e], ln_g[l, 0], ln_b[l, 0], alpha, tm)
        else:
            o = l // 2
            x = _pool_mixer(x, mod, w_pool[o], pool_scale[o], ln_g[l, 0], ln_b[l, 0], alpha, tm)
        x = _conv_ffn(x, mod, w_up[l], ffn_conv_w[l], ffn_conv_b[l], w_down[l],
                      ln_g[l, 1], ln_b[l, 1], alpha, tm)
    return x
```

```python
import functools

import numpy as np
import jax
import jax.numpy as jnp
from jax import lax
from jax.experimental import pallas as pl
from jax.experimental.pallas import tpu as pltpu

F32 = jnp.float32
BF16 = jnp.bfloat16

A_HEADS = 4
A_DK = 128
A_DV = 128
A_KD = A_HEADS * A_DK
A_WIDTH = A_HEADS * A_DV
A_CHUNK = 64
B_Q_HEADS = 8
B_KV_HEADS = 2
B_HEAD_DIM = 64
B_WIDTH = B_Q_HEADS * B_HEAD_DIM
B_KV_WIDTH = B_KV_HEADS * B_HEAD_DIM
WINDOW = 128
ROPE_DIM = B_HEAD_DIM // 4
ROPE_THETA = 500000.0
POOL_WINDOWS = (2, 4, 8, 16)
POOL_HALO = 16
CONV_WIDTH = 3
LN_EPS = 1e-5
RMS_EPS = 1e-6

LANES = 128
SUBLANES = 8
VMEM_LIMIT_BYTES = 56 * 1024 * 1024

NEG_BIG = -1e30
HGRN_FACTORED_MAX_DECAY = 80.0


def _cparams(n_axes):
    return pltpu.CompilerParams(
        dimension_semantics=("arbitrary",) * n_axes,
        vmem_limit_bytes=VMEM_LIMIT_BYTES,
    )


def _dot(a, b):
    return jnp.dot(a, b, preferred_element_type=F32)


def _dot_nt(a, b):
    return lax.dot_general(a, b, (((1,), (1,)), ((), ())), preferred_element_type=F32)


def _dot_tn(a, b):
    return lax.dot_general(a, b, (((0,), (0,)), ((), ())), preferred_element_type=F32)


def _layer_norm(r, g, b):
    mu = jnp.mean(r, axis=-1, keepdims=True)
    d = r - mu
    var = jnp.mean(d * d, axis=-1, keepdims=True)
    return d * lax.rsqrt(var + LN_EPS) * g + b


def _sigmoid_pair(z):
    e = jnp.exp(-jnp.abs(z))
    r = 1.0 / (1.0 + e)
    er = e * r
    pos = z >= 0
    return jnp.where(pos, r, er), jnp.where(pos, er, r)


def _silu(z):
    s, _ = _sigmoid_pair(z)
    return z * s


def _ada_kernel(c_ref, w_ref, b_ref, o_ref):
    c = c_ref[...]
    ca = _silu(c).astype(BF16)
    o_ref[0] = _dot(ca, w_ref[0].astype(BF16)) + b_ref[0]


def _ada_mod(c, ada_w, ada_b, tn=1536):
    depth, d, n = ada_w.shape
    bsz = c.shape[0]
    assert n % tn == 0
    return pl.pallas_call(
        _ada_kernel,
        grid=(depth, n // tn),
        in_specs=[
            pl.BlockSpec((bsz, d), lambda l, j: (0, 0)),
            pl.BlockSpec((1, d, tn), lambda l, j: (l, 0, j)),
            pl.BlockSpec((1, 1, tn), lambda l, j: (l, 0, j)),
        ],
        out_specs=pl.BlockSpec((1, bsz, tn), lambda l, j: (l, 0, j)),
        out_shape=jax.ShapeDtypeStruct((depth, bsz, n), F32),
        compiler_params=_cparams(2),
        name="ada_mod",
    )(c, ada_w, ada_b.reshape(depth, 1, n))


def _inproj_kernel(x_ref, mod_ref, w_ref, za_ref, zb_ref, *, na):
    x = x_ref[0]
    mod = mod_ref[0]
    h = (x * (1.0 + mod[1:2]) + mod[0:1]).astype(BF16)
    z = _dot(h, w_ref[...])
    za_ref[0] = z[:, :na]
    zb_ref[0] = z[:, na:]


def _inproj(x, mod, w_in, tm):
    bsz, t, d = x.shape
    n = w_in.shape[1]
    na = 2 * A_KD + 2 * A_WIDTH
    nb = n - na
    return pl.pallas_call(
        functools.partial(_inproj_kernel, na=na),
        grid=(bsz, t // tm),
        in_specs=[
            pl.BlockSpec((1, tm, d), lambda b, i: (b, i, 0)),
            pl.BlockSpec((1, 6, d), lambda b, i: (b, 0, 0)),
            pl.BlockSpec((d, n), lambda b, i: (0, 0)),
        ],
        out_specs=[
            pl.BlockSpec((1, tm, na), lambda b, i: (b, i, 0)),
            pl.BlockSpec((1, tm, nb), lambda b, i: (b, i, 0)),
        ],
        out_shape=[
            jax.ShapeDtypeStruct((bsz, t, na), F32),
            jax.ShapeDtypeStruct((bsz, t, nb), F32),
        ],
        compiler_params=_cparams(2),
        name="even_inproj",
    )(x, mod, w_in)


def _hgrn_kernel(za_ref, lbraw_ref, ng_ref, o_ref, st_ref, k_ref, bl_ref, *, layer_e, tq):
    ti = pl.program_id(1)

    @pl.when(ti == 0)
    def _():
        st_ref[...] = jnp.zeros_like(st_ref)

    lbraw = lbraw_ref[...]
    lmax = jnp.max(lbraw, axis=0, keepdims=True)
    lexp = jnp.exp(lbraw - lmax)
    lbp = lexp / jnp.sum(lexp, axis=0, keepdims=True)
    lb_all = jnp.zeros((1, A_KD), F32)
    for i in range(1, layer_e + 1):
        lb_all = lb_all + lbp[i:i + 1]

    c = A_CHUNK
    hc = c // 2
    rows_t = lax.broadcasted_iota(jnp.int32, (tq, A_DK), 0)
    rh_t = rows_t & (hc - 1)
    rows = lax.broadcasted_iota(jnp.int32, (c, A_DK), 0)
    upper_half = rows >= hc
    sub = rows & (SUBLANES - 1)
    ri = lax.broadcasted_iota(jnp.int32, (c, c), 0)
    ci = lax.broadcasted_iota(jnp.int32, (c, c), 1)
    ng = ng_ref[...]

    worst = jnp.zeros((tq, A_DK), F32)
    for hd in range(A_HEADS):
        lo = hd * A_DK
        lb = lb_all[:, lo:lo + A_DK]
        sp, sn = _sigmoid_pair(za_ref[0, :, A_KD + lo:A_KD + lo + A_DK])
        k_ref[hd] = (1.0 - lb) * sn
        bl = jnp.log(lb + (1.0 - lb) * sp)
        s = 1
        while s < hc:
            bl = bl + jnp.where(rh_t >= s, pltpu.roll(bl, s, 0), 0.0)
            s *= 2
        bl_ref[hd] = bl
        worst = jnp.maximum(worst, -bl)
    safe = jnp.max(worst) <= HGRN_FACTORED_MAX_DECAY

    def finish(hd, r0, o, st_new):
        st_ref[hd] = st_new
        ag = za_ref[0, r0:r0 + c, 2 * A_KD + A_WIDTH + hd * A_DV:2 * A_KD + A_WIDTH + (hd + 1) * A_DV]
        o = o * lax.rsqrt(jnp.mean(o * o, axis=-1, keepdims=True) + RMS_EPS) * ng
        o_ref[0, r0:r0 + c, hd * A_DV:(hd + 1) * A_DV] = o * (ag * (0.5 * jnp.tanh(0.5 * ag) + 0.5))

    def chunk_inputs(hd, r0):
        qc = za_ref[0, r0:r0 + c, hd * A_DK:(hd + 1) * A_DK]
        vc = za_ref[0, r0:r0 + c, 2 * A_KD + hd * A_DV:2 * A_KD + (hd + 1) * A_DV]
        return qc, k_ref[hd, r0:r0 + c], vc, bl_ref[hd, r0:r0 + c]

    @pl.when(safe)
    def _():
        for hd in range(A_HEADS):
            for ch in range(tq // c):
                r0 = ch * c
                qc, kc, vc, bl = chunk_inputs(hd, r0)
                st = st_ref[hd]
                e_h0 = jnp.exp(bl[hc - 1:hc])
                e_h1 = jnp.exp(bl[c - 1:c])
                e_all = e_h0 * e_h1
                qd = qc * jnp.exp(bl)
                kd = kc * jnp.exp(-bl)
                qdb = qd.astype(BF16)
                vb = vc.astype(BF16)
                a_in = _dot_nt(qdb, kd.astype(BF16))
                a_x = _dot_nt(qdb, (kd * e_h0).astype(BF16))
                same_half = (ri ^ ci) < hc
                att = jnp.where(same_half & (ci <= ri), a_in,
                                jnp.where((ri >= hc) & (ci < hc), a_x, 0.0))
                o = (_dot_nt(jnp.where(upper_half, qd * e_h0, qd).astype(BF16), st.astype(BF16))
                     + _dot(att.astype(BF16), vb))
                ke = (kd * jnp.where(upper_half, e_h1, e_all)).astype(BF16)
                finish(hd, r0, o, e_all * st + _dot_tn(vb, ke))

    @pl.when(jnp.logical_not(safe))
    def _():
        for hd in range(A_HEADS):
            for ch in range(tq // c):
                r0 = ch * c
                qc, kc, vc, bl = chunk_inputs(hd, r0)
                bc = bl + jnp.where(upper_half, bl[hc - 1:hc], 0.0)
                b_last = bc[c - 1:c]
                st = st_ref[hd]
                vb = vc.astype(BF16)
                o = _dot_nt((qc * jnp.exp(bc)).astype(BF16), st.astype(BF16))

                att = jnp.zeros((c, c), F32)
                m = hc
                while m >= SUBLANES:
                    bref = jnp.concatenate(
                        [jnp.broadcast_to(bc[2 * m * j + m - 1:2 * m * j + m], (2 * m, A_DK))
                         for j in range(c // (2 * m))], axis=0)
                    upper = (rows & m) != 0
                    qm = jnp.where(upper, qc * jnp.exp(jnp.minimum(bc - bref, 0.0)), 0.0)
                    km = jnp.where(upper, 0.0, kc * jnp.exp(jnp.minimum(bref - bc, 0.0)))
                    a = _dot_nt(qm.astype(BF16), km.astype(BF16))
                    att = att + jnp.where((ri ^ ci) < 2 * m, a, 0.0)
                    m //= 2
                o = o + _dot(att.astype(BF16), vb)

                for dd in range(SUBLANES):
                    if dd == 0:
                        pd = qc * kc
                        vs = vc
                    else:
                        ks = pltpu.roll(kc, dd, 0)
                        bs = pltpu.roll(bc, dd, 0)
                        vs = pltpu.roll(vc, dd, 0)
                        pd = qc * ks * jnp.exp(jnp.minimum(bc - bs, 0.0))
                        pd = jnp.where(sub >= dd, pd, 0.0)
                    o = o + jnp.sum(pd, axis=-1, keepdims=True) * vs

                ke = (kc * jnp.exp(b_last - bc)).astype(BF16)
                finish(hd, r0, o, jnp.exp(b_last) * st + _dot_tn(vb, ke))


def _hgrn(za, lb_raw, norm_g, layer_e, tq):
    bsz, t, na = za.shape
    return pl.pallas_call(
        functools.partial(_hgrn_kernel, layer_e=layer_e, tq=tq),
        grid=(bsz, t // tq),
        in_specs=[
            pl.BlockSpec((1, tq, na), lambda b, i: (b, i, 0)),
            pl.BlockSpec(lb_raw.shape, lambda b, i: (0, 0)),
            pl.BlockSpec((1, A_DV), lambda b, i: (0, 0)),
        ],
        out_specs=pl.BlockSpec((1, tq, A_WIDTH), lambda b, i: (b, i, 0)),
        out_shape=jax.ShapeDtypeStruct((bsz, t, A_WIDTH), F32),
        scratch_shapes=[pltpu.VMEM((A_HEADS, A_DV, A_DK), F32),
                        pltpu.VMEM((A_HEADS, tq, A_DK), F32),
                        pltpu.VMEM((A_HEADS, tq, A_DK), F32)],
        compiler_params=_cparams(2),
        name="even_hgrn",
    )(za, lb_raw, norm_g.reshape(1, A_DV))


def _rope_lane_table():
    inv = ROPE_THETA ** (-np.arange(0, ROPE_DIM, 2, dtype=np.float64) / ROPE_DIM)
    half = ROPE_DIM // 2
    tab = np.zeros((1, LANES), np.float32)
    for l in range(LANES):
        dpos = l % B_HEAD_DIM
        if dpos < ROPE_DIM:
            tab[0, l] = inv[dpos % half]
    return tab


def _rope_kernel(pos_ref, invf_ref, cos_ref, sin_ref):
    half = ROPE_DIM // 2
    ang = pos_ref[0] * invf_ref[...]
    dpos = lax.broadcasted_iota(jnp.int32, ang.shape, 1) & (B_HEAD_DIM - 1)
    sinv = jnp.sin(ang)
    cos_ref[0] = jnp.cos(ang)
    sin_ref[0] = jnp.where(dpos < half, -sinv, jnp.where(dpos < ROPE_DIM, sinv, 0.0))


def _rope_tables(pos_b, tm):
    bsz, t, _ = pos_b.shape
    spec = pl.BlockSpec((1, tm, LANES), lambda b, i: (b, i, 0))
    return pl.pallas_call(
        _rope_kernel,
        grid=(bsz, t // tm),
        in_specs=[spec, pl.BlockSpec((1, LANES), lambda b, i: (0, 0))],
        out_specs=[spec, spec],
        out_shape=[jax.ShapeDtypeStruct((bsz, t, LANES), F32)] * 2,
        compiler_params=_cparams(2),
        name="rope_tables",
    )(pos_b, jnp.asarray(_rope_lane_table()))


def _swa_kernel(zb_ref, cos_ref, sin_ref, sink_ref, o_ref, kp_ref, vp_ref, *, tq):
    ti = pl.program_id(1)
    w = WINDOW
    half = ROPE_DIM // 2
    hd = B_HEAD_DIM
    ngrp = B_WIDTH // LANES
    assert 2 * hd == LANES and B_KV_WIDTH == LANES and (B_Q_HEADS // B_KV_HEADS) == 4

    @pl.when(ti == 0)
    def _():
        kp_ref[...] = jnp.zeros_like(kp_ref)
        vp_ref[...] = jnp.zeros_like(vp_ref)

    cosv = cos_ref[0]
    sinv = sin_ref[0]
    lane = lax.broadcasted_iota(jnp.int32, (tq, LANES), 1)
    first_half = (lane & (hd - 1)) < half

    def rope(xx):
        partner = jnp.where(first_half, pltpu.roll(xx, LANES - half, 1), pltpu.roll(xx, half, 1))
        return xx * cosv + partner * sinv

    k_new = rope(zb_ref[0, :, B_WIDTH:B_WIDTH + B_KV_WIDTH])
    v_new = zb_ref[0, :, B_WIDTH + B_KV_WIDTH:B_WIDTH + 2 * B_KV_WIDTH]
    kext = jnp.concatenate([kp_ref[...], k_new], axis=0)
    vext = jnp.concatenate([vp_ref[...], v_new], axis=0)
    kp_ref[...] = k_new[tq - w:]
    vp_ref[...] = v_new[tq - w:]

    lane_e = lax.broadcasted_iota(jnp.int32, kext.shape, 1)
    lo_lanes = lane_e < hd

    def lane_variants(a):
        sw = pltpu.roll(a, hd, 1)
        return [[jnp.where(lo_lanes, a, 0.0).astype(BF16), jnp.where(lo_lanes, 0.0, sw).astype(BF16)],
                [jnp.where(lo_lanes, sw, 0.0).astype(BF16), jnp.where(lo_lanes, 0.0, a).astype(BF16)]]

    kvar = lane_variants(kext)
    vvar = lane_variants(vext)

    qs = [(rope(zb_ref[0, :, j * LANES:(j + 1) * LANES]) * (hd ** -0.5)).astype(BF16)
          for j in range(ngrp)]

    qi = (lax.broadcasted_iota(jnp.int32, (2 * w, 2 * w), 0) & (w - 1)) + w
    ki = lax.broadcasted_iota(jnp.int32, (2 * w, 2 * w), 1)
    rel = qi - ki
    allowed = (rel >= 0) & (rel < w)
    bias = jnp.where(allowed, 0.0, NEG_BIG)
    bias_first = jnp.where(allowed & ((ti > 0) | (ki >= w)), 0.0, NEG_BIG)
    top = lax.broadcasted_iota(jnp.int32, (2 * w, 1), 0) < w
    lo_out = lax.broadcasted_iota(jnp.int32, (2 * w, LANES), 1) < hd
    lo_rows = lax.broadcasted_iota(jnp.int32, (4 * w, LANES), 0) < 2 * w
    lo_cols = lax.broadcasted_iota(jnp.int32, (4 * w, LANES), 1) < hd
    ones_sel = jnp.where(lo_rows == lo_cols, 1.0, 0.0).astype(BF16)

    units = [(j, hk) for j in range(tq // w) for hk in range(B_KV_HEADS)]
    stage1 = []
    for j, hk in units:
        bj = bias_first if j == 0 else bias
        qa = jnp.concatenate([qs[2 * hk][j * w:(j + 1) * w], qs[2 * hk + 1][j * w:(j + 1) * w]],
                             axis=0)
        for par in range(2):
            sink = jnp.where(top, sink_ref[4 * hk + par], sink_ref[4 * hk + 2 + par])
            s = _dot_nt(qa, kvar[hk][par][j * w:(j + 2) * w]) + bj
            mx = jnp.maximum(jnp.max(s, axis=-1, keepdims=True), sink)
            stage1.append((s, mx, sink))
    stage2 = []
    for s, mx, sink in stage1:
        stage2.append((jnp.exp(s - mx).astype(BF16), jnp.exp(sink - mx)))
    for ui, (j, hk) in enumerate(units):
        (p0, e0), (p1, e1) = stage2[2 * ui], stage2[2 * ui + 1]
        pp = jnp.concatenate([p0, p1], axis=1)
        vv = jnp.concatenate([vvar[hk][0][j * w:(j + 2) * w], vvar[hk][1][j * w:(j + 2) * w]],
                             axis=0)
        den = _dot(pp, ones_sel) + jnp.where(lo_out, e0, e1)
        o = _dot(pp, vv) / den
        o_ref[0, j * w:(j + 1) * w, 2 * hk * LANES:(2 * hk + 1) * LANES] = o[:w]
        o_ref[0, j * w:(j + 1) * w, (2 * hk + 1) * LANES:(2 * hk + 2) * LANES] = o[w:]


def _swa(zb, cos_t, sin_t, sinks, tq):
    bsz, t, nb = zb.shape
    w = WINDOW
    assert tq % w == 0
    return pl.pallas_call(
        functools.partial(_swa_kernel, tq=tq),
        grid=(bsz, t // tq),
        in_specs=[
            pl.BlockSpec((1, tq, nb), lambda b, i: (b, i, 0)),
            pl.BlockSpec((1, tq, LANES), lambda b, i: (b, i, 0)),
            pl.BlockSpec((1, tq, LANES), lambda b, i: (b, i, 0)),
            pl.BlockSpec(memory_space=pltpu.SMEM),
        ],
        out_specs=pl.BlockSpec((1, tq, B_WIDTH), lambda b, i: (b, i, 0)),
        out_shape=jax.ShapeDtypeStruct((bsz, t, B_WIDTH), F32),
        scratch_shapes=[pltpu.VMEM((w, B_KV_WIDTH), F32), pltpu.VMEM((w, B_KV_WIDTH), F32)],
        compiler_params=_cparams(2),
        name="even_swa",
    )(zb, cos_t, sin_t, sinks)


def _outproj_kernel(x_ref, mod_ref, oa_ref, ob_ref, w_ref, lng_ref, lnb_ref, o_ref, *, alpha):
    x = x_ref[0]
    mod = mod_ref[0]
    wa = oa_ref.shape[-1]
    y = _dot(oa_ref[0].astype(BF16), w_ref[:wa]) + _dot(ob_ref[0].astype(BF16), w_ref[wa:])
    o_ref[0] = _layer_norm(alpha * x + mod[2:3] * y, lng_ref[...], lnb_ref[...])


def _outproj(x, mod, oa, ob, w_out, ln_g, ln_b, alpha, tm):
    bsz, t, d = x.shape
    wa, wb = oa.shape[-1], ob.shape[-1]
    return pl.pallas_call(
        functools.partial(_outproj_kernel, alpha=alpha),
        grid=(bsz, t // tm),
        in_specs=[
            pl.BlockSpec((1, tm, d), lambda b, i: (b, i, 0)),
            pl.BlockSpec((1, 6, d), lambda b, i: (b, 0, 0)),
            pl.BlockSpec((1, tm, wa), lambda b, i: (b, i, 0)),
            pl.BlockSpec((1, tm, wb), lambda b, i: (b, i, 0)),
            pl.BlockSpec((wa + wb, d), lambda b, i: (0, 0)),
            pl.BlockSpec((1, d), lambda b, i: (0, 0)),
            pl.BlockSpec((1, d), lambda b, i: (0, 0)),
        ],
        out_specs=pl.BlockSpec((1, tm, d), lambda b, i: (b, i, 0)),
        out_shape=jax.ShapeDtypeStruct((bsz, t, d), F32),
        compiler_params=_cparams(2),
        name="even_outproj_ln",
    )(x, mod, oa, ob, w_out, ln_g.reshape(1, d), ln_b.reshape(1, d))


def _pool_kernel(x_ref, mod_ref, w_ref, scale_ref, lng_ref, lnb_ref, o_ref, carry_ref, y_ref,
                 *, alpha, tm):
    ti = pl.program_id(1)

    @pl.when(ti == 0)
    def _():
        carry_ref[...] = jnp.zeros_like(carry_ref)

    x = x_ref[0]
    mod = mod_ref[0]
    h = x * (1.0 + mod[1:2]) + mod[0:1]
    ext = jnp.concatenate([carry_ref[...], h], axis=0)
    carry_ref[...] = h[tm - POOL_HALO:]

    t_glob = ti * tm + lax.broadcasted_iota(jnp.int32, (tm, 1), 0)
    gw = x.shape[-1] // len(POOL_WINDOWS)
    for gi, win in enumerate(POOL_WINDOWS):
        e = ext[:, gi * gw:(gi + 1) * gw]
        s = 1
        while s < win:
            e = e + pltpu.roll(e, s, 0)
            s *= 2
        cnt = jnp.minimum(t_glob + 1, win).astype(F32)
        pooled = e[POOL_HALO:] / cnt - h[:, gi * gw:(gi + 1) * gw]
        y_ref[:, gi * gw:(gi + 1) * gw] = _dot(pooled.astype(BF16), w_ref[gi])
    y = y_ref[...] * scale_ref[...]
    o_ref[0] = _layer_norm(alpha * x + mod[2:3] * y, lng_ref[...], lnb_ref[...])


def _pool_mixer(x, mod, w_grp, scale, ln_g, ln_b, alpha, tm):
    bsz, t, d = x.shape
    ng, gw, _ = w_grp.shape
    return pl.pallas_call(
        functools.partial(_pool_kernel, alpha=alpha, tm=tm),
        grid=(bsz, t // tm),
        in_specs=[
            pl.BlockSpec((1, tm, d), lambda b, i: (b, i, 0)),
            pl.BlockSpec((1, 6, d), lambda b, i: (b, 0, 0)),
            pl.BlockSpec((ng, gw, gw), lambda b, i: (0, 0, 0)),
            pl.BlockSpec((1, d), lambda b, i: (0, 0)),
            pl.BlockSpec((1, d), lambda b, i: (0, 0)),
            pl.BlockSpec((1, d), lambda b, i: (0, 0)),
        ],
        out_specs=pl.BlockSpec((1, tm, d), lambda b, i: (b, i, 0)),
        out_shape=jax.ShapeDtypeStruct((bsz, t, d), F32),
        scratch_shapes=[pltpu.VMEM((POOL_HALO, d), F32), pltpu.VMEM((tm, d), F32)],
        compiler_params=_cparams(2),
        name="odd_pool_ln",
    )(x, mod, w_grp, scale.reshape(1, d), ln_g.reshape(1, d), ln_b.reshape(1, d))


def _ffn_kernel(x_ref, mod_ref, wup_ref, cw_ref, cb_ref, wdn_ref, lng_ref, lnb_ref, o_ref,
                tail_ref, ubuf_ref, acc_ref, *, alpha, tm, fc, dff, group):
    ti = pl.program_id(1)
    nslab = fc // LANES
    nchunk = dff // fc

    @pl.when(ti == 0)
    def _():
        tail_ref[...] = jnp.zeros_like(tail_ref)

    x = x_ref[0]
    mod = mod_ref[0]
    h = (x * (1.0 + mod[4:5]) + mod[3:4]).astype(BF16)

    def up(ci):
        lo = ci * fc
        return (_dot(h, wup_ref[:, lo:lo + fc]), _dot(h, wup_ref[:, dff + lo:dff + lo + fc]))

    uv = up(0)
    for g0 in range(0, nchunk, group):
        chunks = range(g0, min(g0 + group, nchunk))
        parts = []
        for ci in chunks:
            u, v = uv
            if ci + 1 < nchunk:
                uv = up(ci + 1)
            for s in range(nslab):
                slab = ci * nslab + s
                cl = ci * fc + s * LANES
                us = u[:, s * LANES:(s + 1) * LANES]
                buf = ubuf_ref.at[ci % 2, s]
                buf[0:SUBLANES, :] = tail_ref[slab]
                buf[SUBLANES:SUBLANES + tm, :] = us
                u1 = buf[SUBLANES - 1:SUBLANES - 1 + tm, :]
                u2 = buf[SUBLANES - 2:SUBLANES - 2 + tm, :]
                tail_ref[slab] = us[tm - SUBLANES:]
                cw = cw_ref[:, cl:cl + LANES]
                uc = cb_ref[:, cl:cl + LANES] + u2 * cw[0:1] + u1 * cw[1:2] + us * cw[2:3]
                sg = 0.5 * jnp.tanh(0.5 * uc) + 0.5
                parts.append((uc * sg * v[:, s * LANES:(s + 1) * LANES]).astype(BF16))
        gact = jnp.concatenate(parts, axis=1)
        part = _dot(gact, wdn_ref[g0 * fc:(g0 + len(chunks)) * fc, :])
        if g0 == 0:
            acc_ref[...] = part
        else:
            acc_ref[...] += part
    o_ref[0] = _layer_norm(alpha * x + mod[5:6] * acc_ref[...], lng_ref[...], lnb_ref[...])


def _conv_ffn(x, mod, w_up, conv_w, conv_b, w_down, ln_g, ln_b, alpha, tm, fc=256, group=4):
    bsz, t, d = x.shape
    dff = w_down.shape[0]
    assert dff % fc == 0 and fc % LANES == 0 and CONV_WIDTH == 3
    return pl.pallas_call(
        functools.partial(_ffn_kernel, alpha=alpha, tm=tm, fc=fc, dff=dff, group=group),
        grid=(bsz, t // tm),
        in_specs=[
            pl.BlockSpec((1, tm, d), lambda b, i: (b, i, 0)),
            pl.BlockSpec((1, 6, d), lambda b, i: (b, 0, 0)),
            pl.BlockSpec((d, 2 * dff), lambda b, i: (0, 0)),
            pl.BlockSpec((CONV_WIDTH, dff), lambda b, i: (0, 0)),
            pl.BlockSpec((1, dff), lambda b, i: (0, 0)),
            pl.BlockSpec((dff, d), lambda b, i: (0, 0)),
            pl.BlockSpec((1, d), lambda b, i: (0, 0)),
            pl.BlockSpec((1, d), lambda b, i: (0, 0)),
        ],
        out_specs=pl.BlockSpec((1, tm, d), lambda b, i: (b, i, 0)),
        out_shape=jax.ShapeDtypeStruct((bsz, t, d), F32),
        scratch_shapes=[pltpu.VMEM((dff // LANES, SUBLANES, LANES), F32),
                        pltpu.VMEM((2, fc // LANES, tm + SUBLANES, LANES), F32),
                        pltpu.VMEM((tm, d), F32)],
        compiler_params=_cparams(2),
        name="ffn_ln",
    )(x, mod, w_up, conv_w, conv_b.reshape(1, dff), w_down, ln_g.reshape(1, d), ln_b.reshape(1, d))


def _pick_tile(t, pref):
    tm = min(t, pref)
    assert t % tm == 0
    return tm


def kernel(x, c, positions, ada_w, ada_b, ln_g, ln_b, ab_w_in, ab_w_out, hgrn_lb_raw, hgrn_norm_g,
           attn_sinks, pool_w, pool_scale, ffn_w_up, ffn_conv_w, ffn_conv_b, ffn_w_down):
    bsz, t, d = x.shape
    depth = ada_w.shape[0]
    alpha = (2.0 * depth) ** 0.25
    tm = _pick_tile(t, 512)
    tq = _pick_tile(t, 256)

    mod_all = _ada_mod(c, ada_w, ada_b).reshape(depth, bsz, 6, d)
    pos_b = jnp.broadcast_to(positions.astype(F32)[:, :, None], (bsz, t, LANES))
    cos_t, sin_t = _rope_tables(pos_b, tm)
    w_in = ab_w_in.astype(BF16)
    w_out = ab_w_out.astype(BF16)
    w_pool = pool_w.astype(BF16)
    w_up = ffn_w_up.astype(BF16)
    w_down = ffn_w_down.astype(BF16)

    for l in range(depth):
        mod = mod_all[l]
        if l % 2 == 0:
            e = l // 2
            za, zb = _inproj(x, mod, w_in[e], tm)
            oa = _hgrn(za, hgrn_lb_raw, hgrn_norm_g[e], e, tq)
            ob = _swa(zb, cos_t, sin_t, attn_sinks[e], tq)
            x = _outproj(x, mod, oa, ob, w_out[e], ln_g[l, 0], ln_b[l, 0], alpha, tm)
        else:
            o = l // 2
            x = _pool_mixer(x, mod, w_pool[o], pool_scale[o], ln_g[l, 0], ln_b[l, 0], alpha, tm)
        x = _conv_ffn(x, mod, w_up[l], ffn_conv_w[l], ffn_conv_b[l], w_down[l],
                      ln_g[l, 1], ln_b[l, 1], alpha, tm)
    return x
```

```python
import functools

import numpy as np
import jax
import jax.numpy as jnp
from jax import lax
from jax.experimental import pallas as pl
from jax.experimental.pallas import tpu as pltpu

F32 = jnp.float32
BF16 = jnp.bfloat16

A_HEADS = 4
A_DK = 128
A_DV = 128
A_KD = A_HEADS * A_DK
A_WIDTH = A_HEADS * A_DV
A_CHUNK = 64
B_Q_HEADS = 8
B_KV_HEADS = 2
B_HEAD_DIM = 64
B_WIDTH = B_Q_HEADS * B_HEAD_DIM
B_KV_WIDTH = B_KV_HEADS * B_HEAD_DIM
WINDOW = 128
ROPE_DIM = B_HEAD_DIM // 4
ROPE_THETA = 500000.0
POOL_WINDOWS = (2, 4, 8, 16)
POOL_HALO = 16
CONV_WIDTH = 3
LN_EPS = 1e-5
RMS_EPS = 1e-6

LANES = 128
SUBLANES = 8
VMEM_LIMIT_BYTES = 56 * 1024 * 1024

EVEN_TILE = 256
ROW_TILE = 512

NEG_BIG = -1e30
HGRN_FACTORED_MAX_DECAY = 85.0


def _cparams(n_axes):
    return pltpu.CompilerParams(
        dimension_semantics=("arbitrary",) * n_axes,
        vmem_limit_bytes=VMEM_LIMIT_BYTES,
    )


def _dot(a, b):
    return jnp.dot(a, b, preferred_element_type=F32)


def _dot_nt(a, b):
    return lax.dot_general(a, b, (((1,), (1,)), ((), ())), preferred_element_type=F32)


def _dot_tn(a, b):
    return lax.dot_general(a, b, (((0,), (0,)), ((), ())), preferred_element_type=F32)


def _layer_norm(r, g, b):
    mu = jnp.mean(r, axis=-1, keepdims=True)
    d = r - mu
    var = jnp.mean(d * d, axis=-1, keepdims=True)
    return d * lax.rsqrt(var + LN_EPS) * g + b


def _sigmoid_pair(z):
    e = jnp.exp(-jnp.abs(z))
    r = 1.0 / (1.0 + e)
    er = e * r
    pos = z >= 0
    return jnp.where(pos, r, er), jnp.where(pos, er, r)


def _silu(z):
    return z * (0.5 * jnp.tanh(0.5 * z) + 0.5)


def _row_spec(tm, width):
    return pl.BlockSpec((1, tm, width), lambda b, i: (b, i, 0))


def _layer_spec(shape, layer):
    zeros = (0,) * len(shape)
    return pl.BlockSpec((None,) + tuple(shape), lambda b, i: (layer,) + zeros)


def _mod_spec(layer, d):
    return pl.BlockSpec((None, 1, 6, d), lambda b, i: (layer, b, 0, 0))


def _ada_kernel(c_ref, w_ref, b_ref, o_ref):
    ca = _silu(c_ref[...]).astype(BF16)
    o_ref[0] = _dot(ca, w_ref[0].astype(BF16)) + b_ref[0]


def _ada_mod(c, ada_w, ada_b, tn=1536):
    depth, d, n = ada_w.shape
    bsz = c.shape[0]
    assert n % tn == 0
    return pl.pallas_call(
        _ada_kernel,
        grid=(depth, n // tn),
        in_specs=[
            pl.BlockSpec((bsz, d), lambda l, j: (0, 0)),
            pl.BlockSpec((1, d, tn), lambda l, j: (l, 0, j)),
            pl.BlockSpec((1, 1, tn), lambda l, j: (l, 0, j)),
        ],
        out_specs=pl.BlockSpec((1, bsz, tn), lambda l, j: (l, 0, j)),
        out_shape=jax.ShapeDtypeStruct((depth, bsz, n), F32),
        compiler_params=_cparams(2),
        name="ada_mod",
    )(c, ada_w, ada_b.reshape(depth, 1, n))


def _rope_kernel(pos_ref, invf_ref, cos_ref, sin_ref, nsin_ref):
    ang = pos_ref[...] * invf_ref[...]
    sinv = jnp.sin(ang)
    cos_ref[...] = jnp.cos(ang)
    sin_ref[...] = sinv
    nsin_ref[...] = -sinv


def _rope_tables(positions):
    bsz, t = positions.shape
    nf = ROPE_DIM // 2
    per_row = LANES // nf
    rows = bsz * t // per_row
    assert (bsz * t) % per_row == 0 and 2 * B_HEAD_DIM == LANES
    inv = ROPE_THETA ** (-np.arange(0, ROPE_DIM, 2, dtype=np.float64) / ROPE_DIM)
    invf = jnp.asarray(np.tile(inv, per_row).astype(np.float32).reshape(1, LANES))
    pos_c = jnp.repeat(positions.astype(F32).reshape(rows, per_row), nf, axis=1)
    tr = _pick_tile(rows, 1024)
    spec = pl.BlockSpec((tr, LANES), lambda i: (i, 0))
    cos_c, sin_c, nsin_c = pl.pallas_call(
        _rope_kernel,
        grid=(rows // tr,),
        in_specs=[spec, pl.BlockSpec((1, LANES), lambda i: (0, 0))],
        out_specs=[spec] * 3,
        out_shape=[jax.ShapeDtypeStruct((rows, LANES), F32)] * 3,
        compiler_params=_cparams(1),
        name="rope_tables",
    )(pos_c, invf)
    cos8, sin8, nsin8 = (a.reshape(bsz, t, nf) for a in (cos_c, sin_c, nsin_c))
    rest = B_HEAD_DIM - ROPE_DIM
    cos_h = jnp.concatenate([cos8, cos8, jnp.ones((bsz, t, rest), F32)], axis=-1)
    sin_h = jnp.concatenate([nsin8, sin8, jnp.zeros((bsz, t, rest), F32)], axis=-1)
    return jnp.concatenate([cos_h, cos_h], axis=-1), jnp.concatenate([sin_h, sin_h], axis=-1)


def _hgrn_prepare(za_ref, lbraw_ref, k_ref, bl_ref, *, layer_e, tq):
    lbraw = lbraw_ref[...]
    lmax = jnp.max(lbraw, axis=0, keepdims=True)
    lexp = jnp.exp(lbraw - lmax)
    lbp = lexp / jnp.sum(lexp, axis=0, keepdims=True)
    lb_all = jnp.zeros((1, A_KD), F32)
    for i in range(1, layer_e + 1):
        lb_all = lb_all + lbp[i:i + 1]

    hc = A_CHUNK // 2
    rh_t = lax.broadcasted_iota(jnp.int32, (tq, A_DK), 0) & (hc - 1)
    worst = jnp.zeros((tq, A_DK), F32)
    for hd in range(A_HEADS):
        lo = hd * A_DK
        lb = lb_all[:, lo:lo + A_DK]
        sp, sn = _sigmoid_pair(za_ref[:, A_KD + lo:A_KD + lo + A_DK])
        k_ref[hd] = (1.0 - lb) * sn
        bl = jnp.log(lb + (1.0 - lb) * sp)
        s = 1
        while s < hc:
            bl = bl + jnp.where(rh_t >= s, pltpu.roll(bl, s, 0), 0.0)
            s *= 2
        bl_ref[hd] = bl
        worst = jnp.maximum(worst, -bl)
    return jnp.max(worst) <= HGRN_FACTORED_MAX_DECAY


def _hgrn_chunks(factored, za_ref, ng_ref, st_ref, k_ref, bl_ref, out_ref, *, tq):
    c = A_CHUNK
    hc = c // 2
    rows = lax.broadcasted_iota(jnp.int32, (c, A_DK), 0)
    upper_half = rows >= hc
    sub = rows & (SUBLANES - 1)
    ri = lax.broadcasted_iota(jnp.int32, (c, c), 0)
    ci = lax.broadcasted_iota(jnp.int32, (c, c), 1)
    ng = ng_ref[...]

    def inputs(ch, hd):
        r0 = ch * c
        qc = za_ref[r0:r0 + c, hd * A_DK:(hd + 1) * A_DK]
        vc = za_ref[r0:r0 + c, 2 * A_KD + hd * A_DV:2 * A_KD + (hd + 1) * A_DV]
        return qc, vc, k_ref[hd, r0:r0 + c], bl_ref[hd, r0:r0 + c]

    def emit(ch, hd, o):
        r0 = ch * c
        ag = za_ref[r0:r0 + c, 2 * A_KD + A_WIDTH + hd * A_DV:2 * A_KD + A_WIDTH + (hd + 1) * A_DV]
        o = o * lax.rsqrt(jnp.mean(o * o, axis=-1, keepdims=True) + RMS_EPS) * ng
        out_ref[r0:r0 + c, hd * A_DV:(hd + 1) * A_DV] = (o * _silu(ag)).astype(out_ref.dtype)

    units = [(ch, hd) for ch in range(tq // c) for hd in range(A_HEADS)]

    if factored:
        for ch, hd in units:
            qc, vc, kc, bl = inputs(ch, hd)
            st = st_ref[hd]
            e_h0 = jnp.exp(bl[hc - 1:hc])
            e_h1 = jnp.exp(bl[c - 1:c])
            e_all = e_h0 * e_h1
            qd = qc * jnp.exp(bl)
            kd = kc * jnp.exp(-bl)
            qdb = qd.astype(BF16)
            vb = vc.astype(BF16)
            a_in = _dot_nt(qdb, kd.astype(BF16))
            a_x = _dot_nt(qdb, (kd * e_h0).astype(BF16))
            att = jnp.where(((ri ^ ci) < hc) & (ci <= ri), a_in,
                            jnp.where((ri >= hc) & (ci < hc), a_x, 0.0))
            o = (_dot_nt(jnp.where(upper_half, qd * e_h0, qd).astype(BF16), st.astype(BF16))
                 + _dot(att.astype(BF16), vb))
            ke = (kd * jnp.where(upper_half, e_h1, e_all)).astype(BF16)
            st_ref[hd] = e_all * st + _dot_tn(vb, ke)
            emit(ch, hd, o)
        return

    for ch, hd in units:
        qc, vc, kc, bl = inputs(ch, hd)
        st = st_ref[hd]
        vb = vc.astype(BF16)
        bc = bl + jnp.where(upper_half, bl[hc - 1:hc], 0.0)
        b_last = bc[c - 1:c]
        o = _dot_nt((qc * jnp.exp(bc)).astype(BF16), st.astype(BF16))
        att = jnp.zeros((c, c), F32)
        m = hc
        while m >= SUBLANES:
            bref = jnp.concatenate(
                [jnp.broadcast_to(bc[2 * m * j + m - 1:2 * m * j + m], (2 * m, A_DK))
                 for j in range(c // (2 * m))], axis=0)
            upper = (rows & m) != 0
            qm = jnp.where(upper, qc * jnp.exp(jnp.minimum(bc - bref, 0.0)), 0.0)
            km = jnp.where(upper, 0.0, kc * jnp.exp(jnp.minimum(bref - bc, 0.0)))
            a = _dot_nt(qm.astype(BF16), km.astype(BF16))
            att = att + jnp.where((ri ^ ci) < 2 * m, a, 0.0)
            m //= 2
        o = o + _dot(att.astype(BF16), vb)
        for dd in range(SUBLANES):
            if dd == 0:
                pd = qc * kc
                vs = vc
            else:
                ks = pltpu.roll(kc, dd, 0)
                bs = pltpu.roll(bc, dd, 0)
                vs = pltpu.roll(vc, dd, 0)
                pd = qc * ks * jnp.exp(jnp.minimum(bc - bs, 0.0))
                pd = jnp.where(sub >= dd, pd, 0.0)
            o = o + jnp.sum(pd, axis=-1, keepdims=True) * vs
        ke = (kc * jnp.exp(b_last - bc)).astype(BF16)
        st_ref[hd] = jnp.exp(b_last) * st + _dot_tn(vb, ke)
        emit(ch, hd, o)


def _swa_tile(zb_ref, cos_ref, sin_ref, sink_ref, kp_ref, vp_ref, out_ref, *, layer_e, ti, tq, col0):
    w = WINDOW
    half = ROPE_DIM // 2
    hd = B_HEAD_DIM
    ngrp = B_WIDTH // LANES
    assert 2 * hd == LANES and B_KV_WIDTH == LANES and (B_Q_HEADS // B_KV_HEADS) == 4

    cosv = cos_ref[0]
    sinv = sin_ref[0]
    lane = lax.broadcasted_iota(jnp.int32, (tq, LANES), 1)
    first_half = (lane & (hd - 1)) < half

    def rope(xx):
        partner = jnp.where(first_half, pltpu.roll(xx, LANES - half, 1), pltpu.roll(xx, half, 1))
        return xx * cosv + partner * sinv

    k_new = rope(zb_ref[:, B_WIDTH:B_WIDTH + B_KV_WIDTH])
    v_new = zb_ref[:, B_WIDTH + B_KV_WIDTH:B_WIDTH + 2 * B_KV_WIDTH]
    kext = jnp.concatenate([kp_ref[...], k_new], axis=0)
    vext = jnp.concatenate([vp_ref[...], v_new], axis=0)
    kp_ref[...] = k_new[tq - w:]
    vp_ref[...] = v_new[tq - w:]

    lo_lanes = lax.broadcasted_iota(jnp.int32, kext.shape, 1) < hd

    def lane_variants(a):
        sw = pltpu.roll(a, hd, 1)
        return [[jnp.where(lo_lanes, a, 0.0).astype(BF16), jnp.where(lo_lanes, 0.0, sw).astype(BF16)],
                [jnp.where(lo_lanes, sw, 0.0).astype(BF16), jnp.where(lo_lanes, 0.0, a).astype(BF16)]]

    kvar = lane_variants(kext)
    vvar = lane_variants(vext)

    qs = [(rope(zb_ref[:, j * LANES:(j + 1) * LANES]) * (hd ** -0.5)).astype(BF16)
          for j in range(ngrp)]

    qi = (lax.broadcasted_iota(jnp.int32, (2 * w, 2 * w), 0) & (w - 1)) + w
    ki = lax.broadcasted_iota(jnp.int32, (2 * w, 2 * w), 1)
    rel = qi - ki
    allowed = (rel >= 0) & (rel < w)
    bias = jnp.where(allowed, 0.0, NEG_BIG)
    bias_first = jnp.where(allowed & ((ti > 0) | (ki >= w)), 0.0, NEG_BIG)
    top = lax.broadcasted_iota(jnp.int32, (2 * w, 1), 0) < w
    lo_out = lax.broadcasted_iota(jnp.int32, (2 * w, LANES), 1) < hd
    lo_rows = lax.broadcasted_iota(jnp.int32, (4 * w, LANES), 0) < 2 * w
    lo_cols = lax.broadcasted_iota(jnp.int32, (4 * w, LANES), 1) < hd
    ones_sel = jnp.where(lo_rows == lo_cols, 1.0, 0.0).astype(BF16)

    units = [(j, hk) for j in range(tq // w) for hk in range(B_KV_HEADS)]
    stage1 = []
    for j, hk in units:
        bj = bias_first if j == 0 else bias
        qa = jnp.concatenate([qs[2 * hk][j * w:(j + 1) * w], qs[2 * hk + 1][j * w:(j + 1) * w]],
                             axis=0)
        for par in range(2):
            sink = jnp.where(top, sink_ref[layer_e, 4 * hk + par], sink_ref[layer_e, 4 * hk + 2 + par])
            s = _dot_nt(qa, kvar[hk][par][j * w:(j + 2) * w]) + bj
            mx = jnp.maximum(jnp.max(s, axis=-1, keepdims=True), sink)
            stage1.append((s, mx, sink))
    yield
    stage2 = []
    for s, mx, sink in stage1:
        stage2.append((jnp.exp(s - mx).astype(BF16), jnp.exp(sink - mx)))
    yield
    for ui, (j, hk) in enumerate(units):
        (p0, e0), (p1, e1) = stage2[2 * ui], stage2[2 * ui + 1]
        pp = jnp.concatenate([p0, p1], axis=1)
        vv = jnp.concatenate([vvar[hk][0][j * w:(j + 2) * w], vvar[hk][1][j * w:(j + 2) * w]],
                             axis=0)
        den = _dot(pp, ones_sel) + jnp.where(lo_out, e0, e1)
        o = (_dot(pp, vv) / den).astype(out_ref.dtype)
        c0 = col0 + 2 * hk * LANES
        out_ref[j * w:(j + 1) * w, c0:c0 + LANES] = o[:w]
        out_ref[j * w:(j + 1) * w, c0 + LANES:c0 + 2 * LANES] = o[w:]


def _even_kernel(x_ref, mod_ref, win_ref, wout_ref, lbraw_ref, ng_ref, cos_ref, sin_ref, sink_ref,
                 lng_ref, lnb_ref, o_ref,
                 za_ref, zb_ref, oc_ref, st_ref, k_ref, bl_ref, kp_ref, vp_ref,
                 *, layer_e, tq, alpha):
    ti = pl.program_id(1)
    na = 2 * A_KD + 2 * A_WIDTH

    @pl.when(ti == 0)
    def _():
        st_ref[...] = jnp.zeros_like(st_ref)
        kp_ref[...] = jnp.zeros_like(kp_ref)
        vp_ref[...] = jnp.zeros_like(vp_ref)

    mod = mod_ref[0]
    h = (x_ref[0] * (1.0 + mod[1:2]) + mod[0:1]).astype(BF16)

    def project(lo, width, dst_ref, dst_lo):
        dst_ref[:, dst_lo:dst_lo + width] = _dot(h, win_ref[:, lo:lo + width])

    project(na, B_WIDTH + 2 * B_KV_WIDTH, zb_ref, 0)
    project(A_KD, A_KD, za_ref, A_KD)
    swa = _swa_tile(zb_ref, cos_ref, sin_ref, sink_ref, kp_ref, vp_ref, oc_ref,
                    layer_e=layer_e, ti=ti, tq=tq, col0=A_WIDTH)
    next(swa)
    project(0, A_KD, za_ref, 0)
    project(2 * A_KD, A_WIDTH, za_ref, 2 * A_KD)
    factored_ok = _hgrn_prepare(za_ref, lbraw_ref, k_ref, bl_ref, layer_e=layer_e, tq=tq)
    next(swa)
    project(2 * A_KD + A_WIDTH, A_WIDTH, za_ref, 2 * A_KD + A_WIDTH)
    for _ in swa:
        pass

    def finish(factored):
        y = _dot(oc_ref[:, A_WIDTH:], wout_ref[A_WIDTH:, :])
        _hgrn_chunks(factored, za_ref, ng_ref, st_ref, k_ref, bl_ref, oc_ref, tq=tq)
        y = y + _dot(oc_ref[:, :A_WIDTH], wout_ref[:A_WIDTH, :])
        o_ref[0] = _layer_norm(alpha * x_ref[0] + mod[2:3] * y, lng_ref[...], lnb_ref[...])

    pl.when(factored_ok)(functools.partial(finish, True))
    pl.when(jnp.logical_not(factored_ok))(functools.partial(finish, False))


def _even_layer(x, mod_all, w_in, w_out, lb_raw, norm_g, cos_t, sin_t, sinks, ln_g, ln_b,
                *, layer, alpha, tq):
    bsz, t, d = x.shape
    e = layer // 2
    n_in = w_in.shape[-1]
    w = WINDOW
    assert tq % w == 0 and tq % A_CHUNK == 0
    const2 = lambda b, i: (0, 0)
    return pl.pallas_call(
        functools.partial(_even_kernel, layer_e=e, tq=tq, alpha=alpha),
        grid=(bsz, t // tq),
        in_specs=[
            _row_spec(tq, d),
            _mod_spec(layer, d),
            _layer_spec((d, n_in), e),
            _layer_spec((A_WIDTH + B_WIDTH, d), e),
            pl.BlockSpec(lb_raw.shape, const2),
            _layer_spec((1, A_DV), e),
            _row_spec(tq, LANES),
            _row_spec(tq, LANES),
            pl.BlockSpec(memory_space=pltpu.SMEM),
            _layer_spec((1, d), 2 * layer),
            _layer_spec((1, d), 2 * layer),
        ],
        out_specs=_row_spec(tq, d),
        out_shape=jax.ShapeDtypeStruct((bsz, t, d), F32),
        scratch_shapes=[
            pltpu.VMEM((tq, 2 * A_KD + 2 * A_WIDTH), F32),
            pltpu.VMEM((tq, B_WIDTH + 2 * B_KV_WIDTH), F32),
            pltpu.VMEM((tq, A_WIDTH + B_WIDTH), BF16),
            pltpu.VMEM((A_HEADS, A_DV, A_DK), F32),
            pltpu.VMEM((A_HEADS, tq, A_DK), F32),
            pltpu.VMEM((A_HEADS, tq, A_DK), F32),
            pltpu.VMEM((w, B_KV_WIDTH), F32),
            pltpu.VMEM((w, B_KV_WIDTH), F32),
        ],
        compiler_params=_cparams(2),
        name="even_layer",
    )(x, mod_all, w_in, w_out, lb_raw, norm_g, cos_t, sin_t, sinks, ln_g, ln_b)


def _pool_kernel(x_ref, mod_ref, w_ref, scale_ref, lng_ref, lnb_ref, o_ref, carry_ref, y_ref,
                 *, alpha, tm):
    ti = pl.program_id(1)

    @pl.when(ti == 0)
    def _():
        carry_ref[...] = jnp.zeros_like(carry_ref)

    x = x_ref[0]
    mod = mod_ref[0]
    h = x * (1.0 + mod[1:2]) + mod[0:1]
    ext = jnp.concatenate([carry_ref[...], h], axis=0)
    carry_ref[...] = h[tm - POOL_HALO:]

    t_glob = ti * tm + lax.broadcasted_iota(jnp.int32, (tm, 1), 0)
    gw = x.shape[-1] // len(POOL_WINDOWS)
    for gi, win in enumerate(POOL_WINDOWS):
        e = ext[:, gi * gw:(gi + 1) * gw]
        s = 1
        while s < win:
            e = e + pltpu.roll(e, s, 0)
            s *= 2
        inv_cnt = 1.0 / jnp.minimum(t_glob + 1, win).astype(F32)
        pooled = e[POOL_HALO:] * inv_cnt - h[:, gi * gw:(gi + 1) * gw]
        y_ref[:, gi * gw:(gi + 1) * gw] = _dot(pooled.astype(BF16), w_ref[gi])
    y = y_ref[...] * scale_ref[...]
    o_ref[0] = _layer_norm(alpha * x + mod[2:3] * y, lng_ref[...], lnb_ref[...])


def _pool_mixer(x, mod_all, w_grp, scale, ln_g, ln_b, *, layer, alpha, tm):
    bsz, t, d = x.shape
    o = layer // 2
    _, ng, gw, _ = w_grp.shape
    assert POOL_HALO >= max(POOL_WINDOWS) and tm >= POOL_HALO
    return pl.pallas_call(
        functools.partial(_pool_kernel, alpha=alpha, tm=tm),
        grid=(bsz, t // tm),
        in_specs=[
            _row_spec(tm, d),
            _mod_spec(layer, d),
            _layer_spec((ng, gw, gw), o),
            _layer_spec((1, d), o),
            _layer_spec((1, d), 2 * layer),
            _layer_spec((1, d), 2 * layer),
        ],
        out_specs=_row_spec(tm, d),
        out_shape=jax.ShapeDtypeStruct((bsz, t, d), F32),
        scratch_shapes=[pltpu.VMEM((POOL_HALO, d), F32), pltpu.VMEM((tm, d), F32)],
        compiler_params=_cparams(2),
        name="odd_pool_ln",
    )(x, mod_all, w_grp, scale, ln_g, ln_b)


def _ffn_kernel(x_ref, mod_ref, wup_ref, cw_ref, cb_ref, wdn_ref, lng_ref, lnb_ref, o_ref,
                tail_ref, ubuf_ref, acc_ref, *, alpha, tm, fc, dff, group):
    ti = pl.program_id(1)
    nslab = fc // LANES
    nchunk = dff // fc

    @pl.when(ti == 0)
    def _():
        tail_ref[...] = jnp.zeros_like(tail_ref)

    x = x_ref[0]
    mod = mod_ref[0]
    h = (x * (1.0 + mod[4:5]) + mod[3:4]).astype(BF16)

    def up(ci):
        lo = ci * fc
        return (_dot(h, wup_ref[:, lo:lo + fc]), _dot(h, wup_ref[:, dff + lo:dff + lo + fc]))

    uv = up(0)
    for g0 in range(0, nchunk, group):
        chunks = range(g0, min(g0 + group, nchunk))
        parts = []
        for ci in chunks:
            u, v = uv
            if ci + 1 < nchunk:
                uv = up(ci + 1)
            for s in range(nslab):
                slab = ci * nslab + s
                cl = ci * fc + s * LANES
                us = u[:, s * LANES:(s + 1) * LANES]
                buf = ubuf_ref.at[ci % 2, s]
                buf[0:SUBLANES, :] = tail_ref[slab]
                buf[SUBLANES:SUBLANES + tm, :] = us
                u1 = buf[SUBLANES - 1:SUBLANES - 1 + tm, :]
                u2 = buf[SUBLANES - 2:SUBLANES - 2 + tm, :]
                tail_ref[slab] = us[tm - SUBLANES:]
                cw = cw_ref[:, cl:cl + LANES]
                uc = cb_ref[:, cl:cl + LANES] + u2 * cw[0:1] + u1 * cw[1:2] + us * cw[2:3]
                parts.append((_silu(uc) * v[:, s * LANES:(s + 1) * LANES]).astype(BF16))
        gact = jnp.concatenate(parts, axis=1)
        part = _dot(gact, wdn_ref[g0 * fc:(g0 + len(chunks)) * fc, :])
        if g0 == 0:
            acc_ref[...] = part
        else:
            acc_ref[...] += part
    o_ref[0] = _layer_norm(alpha * x + mod[5:6] * acc_ref[...], lng_ref[...], lnb_ref[...])


def _conv_ffn(x, mod_all, w_up, conv_w, conv_b, w_down, ln_g, ln_b, *, layer, alpha, tm,
              fc=256, group=4):
    bsz, t, d = x.shape
    dff = w_down.shape[1]
    assert dff % fc == 0 and fc % LANES == 0 and CONV_WIDTH == 3 and tm >= SUBLANES
    return pl.pallas_call(
        functools.partial(_ffn_kernel, alpha=alpha, tm=tm, fc=fc, dff=dff, group=group),
        grid=(bsz, t // tm),
        in_specs=[
            _row_spec(tm, d),
            _mod_spec(layer, d),
            _layer_spec((d, 2 * dff), layer),
            _layer_spec((CONV_WIDTH, dff), layer),
            _layer_spec((1, dff), layer),
            _layer_spec((dff, d), layer),
            _layer_spec((1, d), 2 * layer + 1),
            _layer_spec((1, d), 2 * layer + 1),
        ],
        out_specs=_row_spec(tm, d),
        out_shape=jax.ShapeDtypeStruct((bsz, t, d), F32),
        scratch_shapes=[pltpu.VMEM((dff // LANES, SUBLANES, LANES), F32),
                        pltpu.VMEM((2, fc // LANES, tm + SUBLANES, LANES), F32),
                        pltpu.VMEM((tm, d), F32)],
        compiler_params=_cparams(2),
        name="ffn_ln",
    )(x, mod_all, w_up, conv_w, conv_b, w_down, ln_g, ln_b)


def _pick_tile(t, pref):
    tm = min(t, pref)
    assert t % tm == 0
    return tm


def kernel(x, c, positions, ada_w, ada_b, ln_g, ln_b, ab_w_in, ab_w_out, hgrn_lb_raw, hgrn_norm_g,
           attn_sinks, pool_w, pool_scale, ffn_w_up, ffn_conv_w, ffn_conv_b, ffn_w_down):
    bsz, t, d = x.shape
    depth = ada_w.shape[0]
    alpha = (2.0 * depth) ** 0.25
    tm = _pick_tile(t, ROW_TILE)
    tq = _pick_tile(t, EVEN_TILE)

    mod_all = _ada_mod(c, ada_w, ada_b).reshape(depth, bsz, 6, d)
    cos_t, sin_t = _rope_tables(positions)
    w_in = ab_w_in.astype(BF16)
    w_out = ab_w_out.astype(BF16)
    w_pool = pool_w.astype(BF16)
    w_up = ffn_w_up.astype(BF16)
    w_down = ffn_w_down.astype(BF16)
    ln_g2 = ln_g.reshape(2 * depth, 1, d)
    ln_b2 = ln_b.reshape(2 * depth, 1, d)
    norm_g = hgrn_norm_g.reshape(-1, 1, A_DV)
    pool_sc = pool_scale.reshape(-1, 1, d)
    conv_b = ffn_conv_b.reshape(depth, 1, -1)

    for l in range(depth):
        if l % 2 == 0:
            x = _even_layer(x, mod_all, w_in, w_out, hgrn_lb_raw, norm_g, cos_t, sin_t, attn_sinks,
                            ln_g2, ln_b2, layer=l, alpha=alpha, tq=tq)
        else:
            x = _pool_mixer(x, mod_all, w_pool, pool_sc, ln_g2, ln_b2, layer=l, alpha=alpha, tm=tm)
        x = _conv_ffn(x, mod_all, w_up, ffn_conv_w, conv_b, w_down, ln_g2, ln_b2,
                      layer=l, alpha=alpha, tm=tm)
    return x
```

```python
import functools

import numpy as np
import jax
import jax.numpy as jnp
from jax import lax
from jax.experimental import pallas as pl
from jax.experimental.pallas import tpu as pltpu

F32 = jnp.float32
BF16 = jnp.bfloat16

A_HEADS = 4
A_DK = 128
A_DV = 128
A_KD = A_HEADS * A_DK
A_WIDTH = A_HEADS * A_DV
A_CHUNK = 64
B_Q_HEADS = 8
B_KV_HEADS = 2
B_HEAD_DIM = 64
B_WIDTH = B_Q_HEADS * B_HEAD_DIM
B_KV_WIDTH = B_KV_HEADS * B_HEAD_DIM
WINDOW = 128
ROPE_DIM = B_HEAD_DIM // 4
ROPE_THETA = 500000.0
POOL_WINDOWS = (2, 4, 8, 16)
POOL_HALO = 16
CONV_WIDTH = 3
LN_EPS = 1e-5
RMS_EPS = 1e-6

LANES = 128
SUBLANES = 8
VMEM_LIMIT_BYTES = 56 * 1024 * 1024

EVEN_TILE = 512
ROW_TILE = 512
FFN_TILE = 1024

NEG_BIG = -1e30
HGRN_FACTORED_MAX_DECAY = 85.0


def _cparams(n_axes):
    return pltpu.CompilerParams(
        dimension_semantics=("arbitrary",) * n_axes,
        vmem_limit_bytes=VMEM_LIMIT_BYTES,
    )


def _dot(a, b):
    return jnp.dot(a, b, preferred_element_type=F32)


def _dot_nt(a, b):
    return lax.dot_general(a, b, (((1,), (1,)), ((), ())), preferred_element_type=F32)


def _dot_tn(a, b):
    return lax.dot_general(a, b, (((0,), (0,)), ((), ())), preferred_element_type=F32)


def _layer_norm(r, g, b):
    mu = jnp.mean(r, axis=-1, keepdims=True)
    d = r - mu
    var = jnp.mean(d * d, axis=-1, keepdims=True)
    return d * lax.rsqrt(var + LN_EPS) * g + b


def _sigmoid_pair(z):
    e = jnp.exp(-jnp.abs(z))
    r = 1.0 / (1.0 + e)
    er = e * r
    pos = z >= 0
    return jnp.where(pos, r, er), jnp.where(pos, er, r)


def _silu(z):
    return z * (0.5 * jnp.tanh(0.5 * z) + 0.5)


def _row_spec(tm, width):
    return pl.BlockSpec((1, tm, width), lambda b, i: (b, i, 0))


def _layer_spec(shape, layer):
    zeros = (0,) * len(shape)
    return pl.BlockSpec((None,) + tuple(shape), lambda b, i: (layer,) + zeros)


def _mod_spec(layer, d):
    return pl.BlockSpec((None, 1, 6, d), lambda b, i: (layer, b, 0, 0))


def _ada_kernel(c_ref, w_ref, b_ref, o_ref):
    ca = _silu(c_ref[...]).astype(BF16)
    o_ref[0] = _dot(ca, w_ref[0].astype(BF16)) + b_ref[0]


def _ada_mod(c, ada_w, ada_b, tn=1536):
    depth, d, n = ada_w.shape
    bsz = c.shape[0]
    assert n % tn == 0
    return pl.pallas_call(
        _ada_kernel,
        grid=(depth, n // tn),
        in_specs=[
            pl.BlockSpec((bsz, d), lambda l, j: (0, 0)),
            pl.BlockSpec((1, d, tn), lambda l, j: (l, 0, j)),
            pl.BlockSpec((1, 1, tn), lambda l, j: (l, 0, j)),
        ],
        out_specs=pl.BlockSpec((1, bsz, tn), lambda l, j: (l, 0, j)),
        out_shape=jax.ShapeDtypeStruct((depth, bsz, n), F32),
        compiler_params=_cparams(2),
        name="ada_mod",
    )(c, ada_w, ada_b.reshape(depth, 1, n))


def _rope_lane_table():
    inv = ROPE_THETA ** (-np.arange(0, ROPE_DIM, 2, dtype=np.float64) / ROPE_DIM)
    half = ROPE_DIM // 2
    tab = np.zeros((1, LANES), np.float32)
    for l in range(LANES):
        dpos = l % B_HEAD_DIM
        if dpos < ROPE_DIM:
            tab[0, l] = inv[dpos % half]
    return tab


def _rope_kernel(pos_ref, invf_ref, cos_ref, sin_ref):
    half = ROPE_DIM // 2
    ang = pos_ref[0] * invf_ref[...]
    dpos = lax.broadcasted_iota(jnp.int32, ang.shape, 1) & (B_HEAD_DIM - 1)
    sinv = jnp.sin(ang)
    cos_ref[0] = jnp.cos(ang)
    sin_ref[0] = jnp.where(dpos < half, -sinv, jnp.where(dpos < ROPE_DIM, sinv, 0.0))


def _rope_tables(positions, tm):
    bsz, t = positions.shape
    pos_b = jnp.broadcast_to(positions.astype(F32)[:, :, None], (bsz, t, LANES))
    spec = _row_spec(tm, LANES)
    return pl.pallas_call(
        _rope_kernel,
        grid=(bsz, t // tm),
        in_specs=[spec, pl.BlockSpec((1, LANES), lambda b, i: (0, 0))],
        out_specs=[spec, spec],
        out_shape=[jax.ShapeDtypeStruct((bsz, t, LANES), F32)] * 2,
        compiler_params=_cparams(2),
        name="rope_tables",
    )(pos_b, jnp.asarray(_rope_lane_table()))


def _hgrn_prepare(za_ref, lbraw_ref, k_ref, bl_ref, *, layer_e, tq):
    lbraw = lbraw_ref[...]
    lmax = jnp.max(lbraw, axis=0, keepdims=True)
    lexp = jnp.exp(lbraw - lmax)
    lbp = lexp / jnp.sum(lexp, axis=0, keepdims=True)
    lb_all = jnp.zeros((1, A_KD), F32)
    for i in range(1, layer_e + 1):
        lb_all = lb_all + lbp[i:i + 1]

    hc = A_CHUNK // 2
    rh_t = lax.broadcasted_iota(jnp.int32, (tq, A_DK), 0) & (hc - 1)
    worst = jnp.zeros((tq, A_DK), F32)
    for hd in range(A_HEADS):
        lo = hd * A_DK
        lb = lb_all[:, lo:lo + A_DK]
        sp, sn = _sigmoid_pair(za_ref[:, A_KD + lo:A_KD + lo + A_DK])
        k_ref[hd] = (1.0 - lb) * sn
        bl = jnp.log(lb + (1.0 - lb) * sp)
        s = 1
        while s < hc:
            bl = bl + jnp.where(rh_t >= s, pltpu.roll(bl, s, 0), 0.0)
            s *= 2
        bl_ref[hd] = bl
        worst = jnp.maximum(worst, -bl)
    return jnp.max(worst) <= HGRN_FACTORED_MAX_DECAY


def _hgrn_chunks(factored, za_ref, ng_ref, st_ref, k_ref, bl_ref, out_ref, *, tq):
    c = A_CHUNK
    hc = c // 2
    rows = lax.broadcasted_iota(jnp.int32, (c, A_DK), 0)
    upper_half = rows >= hc
    sub = rows & (SUBLANES - 1)
    ri = lax.broadcasted_iota(jnp.int32, (c, c), 0)
    ci = lax.broadcasted_iota(jnp.int32, (c, c), 1)
    ng = ng_ref[...]

    def inputs(ch, hd):
        r0 = ch * c
        qc = za_ref[r0:r0 + c, hd * A_DK:(hd + 1) * A_DK]
        vc = za_ref[r0:r0 + c, 2 * A_KD + hd * A_DV:2 * A_KD + (hd + 1) * A_DV]
        return qc, vc, k_ref[hd, r0:r0 + c], bl_ref[hd, r0:r0 + c]

    def emit(ch, hd, o):
        r0 = ch * c
        ag = za_ref[r0:r0 + c, 2 * A_KD + A_WIDTH + hd * A_DV:2 * A_KD + A_WIDTH + (hd + 1) * A_DV]
        o = o * lax.rsqrt(jnp.mean(o * o, axis=-1, keepdims=True) + RMS_EPS) * ng
        out_ref[r0:r0 + c, hd * A_DV:(hd + 1) * A_DV] = (o * _silu(ag)).astype(out_ref.dtype)

    units = [(ch, hd) for ch in range(tq // c) for hd in range(A_HEADS)]

    if factored:
        for ch, hd in units:
            qc, vc, kc, bl = inputs(ch, hd)
            st = st_ref[hd]
            e_h0 = jnp.exp(bl[hc - 1:hc])
            e_h1 = jnp.exp(bl[c - 1:c])
            e_all = e_h0 * e_h1
            qd = qc * jnp.exp(bl)
            kd = kc * jnp.exp(-bl)
            qdb = qd.astype(BF16)
            vb = vc.astype(BF16)
            a_in = _dot_nt(qdb, kd.astype(BF16))
            a_x = _dot_nt(qdb, (kd * e_h0).astype(BF16))
            att = jnp.where(((ri ^ ci) < hc) & (ci <= ri), a_in,
                            jnp.where((ri >= hc) & (ci < hc), a_x, 0.0))
            o = (_dot_nt(jnp.where(upper_half, qd * e_h0, qd).astype(BF16), st.astype(BF16))
                 + _dot(att.astype(BF16), vb))
            ke = (kd * jnp.where(upper_half, e_h1, e_all)).astype(BF16)
            st_ref[hd] = e_all * st + _dot_tn(vb, ke)
            emit(ch, hd, o)
        return

    for ch, hd in units:
        qc, vc, kc, bl = inputs(ch, hd)
        st = st_ref[hd]
        vb = vc.astype(BF16)
        bc = bl + jnp.where(upper_half, bl[hc - 1:hc], 0.0)
        b_last = bc[c - 1:c]
        o = _dot_nt((qc * jnp.exp(bc)).astype(BF16), st.astype(BF16))
        att = jnp.zeros((c, c), F32)
        m = hc
        while m >= SUBLANES:
            bref = jnp.concatenate(
                [jnp.broadcast_to(bc[2 * m * j + m - 1:2 * m * j + m], (2 * m, A_DK))
                 for j in range(c // (2 * m))], axis=0)
            upper = (rows & m) != 0
            qm = jnp.where(upper, qc * jnp.exp(jnp.minimum(bc - bref, 0.0)), 0.0)
            km = jnp.where(upper, 0.0, kc * jnp.exp(jnp.minimum(bref - bc, 0.0)))
            a = _dot_nt(qm.astype(BF16), km.astype(BF16))
            att = att + jnp.where((ri ^ ci) < 2 * m, a, 0.0)
            m //= 2
        o = o + _dot(att.astype(BF16), vb)
        for dd in range(SUBLANES):
            if dd == 0:
                pd = qc * kc
                vs = vc
            else:
                ks = pltpu.roll(kc, dd, 0)
                bs = pltpu.roll(bc, dd, 0)
                vs = pltpu.roll(vc, dd, 0)
                pd = qc * ks * jnp.exp(jnp.minimum(bc - bs, 0.0))
                pd = jnp.where(sub >= dd, pd, 0.0)
            o = o + jnp.sum(pd, axis=-1, keepdims=True) * vs
        ke = (kc * jnp.exp(b_last - bc)).astype(BF16)
        st_ref[hd] = jnp.exp(b_last) * st + _dot_tn(vb, ke)
        emit(ch, hd, o)


def _swa_tile(zb_ref, cos_ref, sin_ref, sink_ref, kp_ref, vp_ref, out_ref, *, layer_e, ti, tq, col0):
    w = WINDOW
    half = ROPE_DIM // 2
    hd = B_HEAD_DIM
    ngrp = B_WIDTH // LANES
    assert 2 * hd == LANES and B_KV_WIDTH == LANES and (B_Q_HEADS // B_KV_HEADS) == 4

    cosv = cos_ref[0]
    sinv = sin_ref[0]
    lane = lax.broadcasted_iota(jnp.int32, (tq, LANES), 1)
    first_half = (lane & (hd - 1)) < half

    def rope(xx):
        partner = jnp.where(first_half, pltpu.roll(xx, LANES - half, 1), pltpu.roll(xx, half, 1))
        return xx * cosv + partner * sinv

    k_new = rope(zb_ref[:, B_WIDTH:B_WIDTH + B_KV_WIDTH])
    v_new = zb_ref[:, B_WIDTH + B_KV_WIDTH:B_WIDTH + 2 * B_KV_WIDTH]
    kext = jnp.concatenate([kp_ref[...], k_new], axis=0)
    vext = jnp.concatenate([vp_ref[...], v_new], axis=0)
    kp_ref[...] = k_new[tq - w:]
    vp_ref[...] = v_new[tq - w:]

    lo_lanes = lax.broadcasted_iota(jnp.int32, kext.shape, 1) < hd

    def lane_variants(a):
        sw = pltpu.roll(a, hd, 1)
        return [[jnp.where(lo_lanes, a, 0.0).astype(BF16), jnp.where(lo_lanes, 0.0, sw).astype(BF16)],
                [jnp.where(lo_lanes, sw, 0.0).astype(BF16), jnp.where(lo_lanes, 0.0, a).astype(BF16)]]

    kvar = lane_variants(kext)
    vvar = lane_variants(vext)

    qs = [(rope(zb_ref[:, j * LANES:(j + 1) * LANES]) * (hd ** -0.5)).astype(BF16)
          for j in range(ngrp)]

    qi = (lax.broadcasted_iota(jnp.int32, (2 * w, 2 * w), 0) & (w - 1)) + w
    ki = lax.broadcasted_iota(jnp.int32, (2 * w, 2 * w), 1)
    rel = qi - ki
    allowed = (rel >= 0) & (rel < w)
    bias = jnp.where(allowed, 0.0, NEG_BIG)
    bias_first = jnp.where(allowed & ((ti > 0) | (ki >= w)), 0.0, NEG_BIG)
    top = lax.broadcasted_iota(jnp.int32, (2 * w, 1), 0) < w
    lo_out = lax.broadcasted_iota(jnp.int32, (2 * w, LANES), 1) < hd
    lo_rows = lax.broadcasted_iota(jnp.int32, (4 * w, LANES), 0) < 2 * w
    lo_cols = lax.broadcasted_iota(jnp.int32, (4 * w, LANES), 1) < hd
    ones_sel = jnp.where(lo_rows == lo_cols, 1.0, 0.0).astype(BF16)

    units = [(j, hk) for j in range(tq // w) for hk in range(B_KV_HEADS)]
    stage1 = []
    for j, hk in units:
        bj = bias_first if j == 0 else bias
        qa = jnp.concatenate([qs[2 * hk][j * w:(j + 1) * w], qs[2 * hk + 1][j * w:(j + 1) * w]],
                             axis=0)
        for par in range(2):
            sink = jnp.where(top, sink_ref[layer_e, 4 * hk + par], sink_ref[layer_e, 4 * hk + 2 + par])
            s = _dot_nt(qa, kvar[hk][par][j * w:(j + 2) * w]) + bj
            mx = jnp.maximum(jnp.max(s, axis=-1, keepdims=True), sink)
            stage1.append((s, mx, sink))
    yield
    stage2 = []
    for s, mx, sink in stage1:
        stage2.append((jnp.exp(s - mx).astype(BF16), jnp.exp(sink - mx)))
    yield
    for ui, (j, hk) in enumerate(units):
        (p0, e0), (p1, e1) = stage2[2 * ui], stage2[2 * ui + 1]
        pp = jnp.concatenate([p0, p1], axis=1)
        vv = jnp.concatenate([vvar[hk][0][j * w:(j + 2) * w], vvar[hk][1][j * w:(j + 2) * w]],
                             axis=0)
        den = _dot(pp, ones_sel) + jnp.where(lo_out, e0, e1)
        o = (_dot(pp, vv) / den).astype(out_ref.dtype)
        c0 = col0 + 2 * hk * LANES
        out_ref[j * w:(j + 1) * w, c0:c0 + LANES] = o[:w]
        out_ref[j * w:(j + 1) * w, c0 + LANES:c0 + 2 * LANES] = o[w:]


def _even_kernel(x_ref, mod_ref, win_ref, wout_ref, lbraw_ref, ng_ref, cos_ref, sin_ref, sink_ref,
                 lng_ref, lnb_ref, o_ref,
                 za_ref, zb_ref, oc_ref, st_ref, k_ref, bl_ref, kp_ref, vp_ref,
                 *, layer_e, tq, alpha):
    ti = pl.program_id(1)
    na = 2 * A_KD + 2 * A_WIDTH

    @pl.when(ti == 0)
    def _():
        st_ref[...] = jnp.zeros_like(st_ref)
        kp_ref[...] = jnp.zeros_like(kp_ref)
        vp_ref[...] = jnp.zeros_like(vp_ref)

    mod = mod_ref[0]
    h = (x_ref[0] * (1.0 + mod[1:2]) + mod[0:1]).astype(BF16)

    def project(lo, width, dst_ref, dst_lo):
        dst_ref[:, dst_lo:dst_lo + width] = _dot(h, win_ref[:, lo:lo + width])

    project(na, B_WIDTH + 2 * B_KV_WIDTH, zb_ref, 0)
    project(A_KD, A_KD, za_ref, A_KD)
    swa = _swa_tile(zb_ref, cos_ref, sin_ref, sink_ref, kp_ref, vp_ref, oc_ref,
                    layer_e=layer_e, ti=ti, tq=tq, col0=A_WIDTH)
    next(swa)
    project(0, A_KD, za_ref, 0)
    project(2 * A_KD, A_WIDTH, za_ref, 2 * A_KD)
    factored_ok = _hgrn_prepare(za_ref, lbraw_ref, k_ref, bl_ref, layer_e=layer_e, tq=tq)
    next(swa)
    project(2 * A_KD + A_WIDTH, A_WIDTH, za_ref, 2 * A_KD + A_WIDTH)
    for _ in swa:
        pass

    def finish(factored):
        y = _dot(oc_ref[:, A_WIDTH:], wout_ref[A_WIDTH:, :])
        _hgrn_chunks(factored, za_ref, ng_ref, st_ref, k_ref, bl_ref, oc_ref, tq=tq)
        y = y + _dot(oc_ref[:, :A_WIDTH], wout_ref[:A_WIDTH, :])
        o_ref[0] = _layer_norm(alpha * x_ref[0] + mod[2:3] * y, lng_ref[...], lnb_ref[...])

    pl.when(factored_ok)(functools.partial(finish, True))
    pl.when(jnp.logical_not(factored_ok))(functools.partial(finish, False))


def _even_layer(x, mod_all, w_in, w_out, lb_raw, norm_g, cos_t, sin_t, sinks, ln_g, ln_b,
                *, layer, alpha, tq):
    bsz, t, d = x.shape
    e = layer // 2
    n_in = w_in.shape[-1]
    w = WINDOW
    assert tq % w == 0 and tq % A_CHUNK == 0
    const2 = lambda b, i: (0, 0)
    return pl.pallas_call(
        functools.partial(_even_kernel, layer_e=e, tq=tq, alpha=alpha),
        grid=(bsz, t // tq),
        in_specs=[
            _row_spec(tq, d),
            _mod_spec(layer, d),
            _layer_spec((d, n_in), e),
            _layer_spec((A_WIDTH + B_WIDTH, d), e),
            pl.BlockSpec(lb_raw.shape, const2),
            _layer_spec((1, A_DV), e),
            _row_spec(tq, LANES),
            _row_spec(tq, LANES),
            pl.BlockSpec(memory_space=pltpu.SMEM),
            _layer_spec((1, d), 2 * layer),
            _layer_spec((1, d), 2 * layer),
        ],
        out_specs=_row_spec(tq, d),
        out_shape=jax.ShapeDtypeStruct((bsz, t, d), F32),
        scratch_shapes=[
            pltpu.VMEM((tq, 2 * A_KD + 2 * A_WIDTH), F32),
            pltpu.VMEM((tq, B_WIDTH + 2 * B_KV_WIDTH), F32),
            pltpu.VMEM((tq, A_WIDTH + B_WIDTH), BF16),
            pltpu.VMEM((A_HEADS, A_DV, A_DK), F32),
            pltpu.VMEM((A_HEADS, tq, A_DK), F32),
            pltpu.VMEM((A_HEADS, tq, A_DK), F32),
            pltpu.VMEM((w, B_KV_WIDTH), F32),
            pltpu.VMEM((w, B_KV_WIDTH), F32),
        ],
        compiler_params=_cparams(2),
        name="even_layer",
    )(x, mod_all, w_in, w_out, lb_raw, norm_g, cos_t, sin_t, sinks, ln_g, ln_b)


def _pool_kernel(x_ref, mod_ref, w_ref, scale_ref, lng_ref, lnb_ref, o_ref, stage_ref, y_ref,
                 *, alpha, tm):
    ti = pl.program_id(1)
    hl = POOL_HALO

    @pl.when(ti == 0)
    def _():
        stage_ref[...] = jnp.zeros_like(stage_ref)

    x = x_ref[0]
    mod = mod_ref[0]
    h = x * (1.0 + mod[1:2]) + mod[0:1]

    t_glob = ti * tm + lax.broadcasted_iota(jnp.int32, (tm, 1), 0)
    gw = x.shape[-1] // len(POOL_WINDOWS)
    for gi, win in enumerate(POOL_WINDOWS):
        inv_cnt = 1.0 / jnp.minimum(t_glob + 1, win).astype(F32)
        sums = []
        for sl in range(gw // LANES):
            slab = gi * (gw // LANES) + sl
            hs = h[:, slab * LANES:(slab + 1) * LANES]
            src, dst = stage_ref.at[0, slab], stage_ref.at[1, slab]
            src[2 * hl:2 * hl + tm, :] = hs
            step = 1
            while True:
                cur = src[hl:2 * hl + tm, :] + src[hl - step:2 * hl + tm - step, :]
                step *= 2
                if step >= win:
                    break
                dst[hl:2 * hl + tm, :] = cur
                src, dst = dst, src
            stage_ref[0, slab, hl:2 * hl, :] = hs[tm - hl:]
            sums.append(cur[hl:])
        pooled = jnp.concatenate(sums, axis=1) * inv_cnt - h[:, gi * gw:(gi + 1) * gw]
        y_ref[:, gi * gw:(gi + 1) * gw] = _dot(pooled.astype(BF16), w_ref[gi])
    y = y_ref[...] * scale_ref[...]
    o_ref[0] = _layer_norm(alpha * x + mod[2:3] * y, lng_ref[...], lnb_ref[...])


def _pool_mixer(x, mod_all, w_grp, scale, ln_g, ln_b, *, layer, alpha, tm):
    bsz, t, d = x.shape
    o = layer // 2
    _, ng, gw, _ = w_grp.shape
    assert POOL_HALO >= max(POOL_WINDOWS) and tm >= POOL_HALO and gw % LANES == 0
    assert all(w & (w - 1) == 0 and w >= 2 for w in POOL_WINDOWS)
    return pl.pallas_call(
        functools.partial(_pool_kernel, alpha=alpha, tm=tm),
        grid=(bsz, t // tm),
        in_specs=[
            _row_spec(tm, d),
            _mod_spec(layer, d),
            _layer_spec((ng, gw, gw), o),
            _layer_spec((1, d), o),
            _layer_spec((1, d), 2 * layer),
            _layer_spec((1, d), 2 * layer),
        ],
        out_specs=_row_spec(tm, d),
        out_shape=jax.ShapeDtypeStruct((bsz, t, d), F32),
        scratch_shapes=[pltpu.VMEM((2, d // LANES, 2 * POOL_HALO + tm, LANES), F32),
                        pltpu.VMEM((tm, d), F32)],
        compiler_params=_cparams(2),
        name="odd_pool_ln",
    )(x, mod_all, w_grp, scale, ln_g, ln_b)


def _ffn_kernel(x_ref, mod_ref, wup_ref, cw_ref, cb_ref, wdn_ref, lng_ref, lnb_ref, o_ref,
                tail_ref, ubuf_ref, acc_ref, *, alpha, tm, sub, fc, dff, group):
    ti = pl.program_id(1)
    nslab = fc // LANES
    nchunk = dff // fc
    nsub = tm // sub

    @pl.when(ti == 0)
    def _():
        tail_ref[...] = jnp.zeros_like(tail_ref)

    mod = mod_ref[0]
    scale = 1.0 + mod[4:5]
    hs = {}

    def up(si, ci):
        if si not in hs:
            hs[si] = (x_ref[0, si * sub:(si + 1) * sub, :] * scale + mod[3:4]).astype(BF16)
        lo = ci * fc
        return (_dot(hs[si], wup_ref[:, lo:lo + fc]),
                _dot(hs[si], wup_ref[:, dff + lo:dff + lo + fc]))

    seq = [(si, ci) for si in range(nsub) for ci in range(nchunk)]
    uv = up(*seq[0])
    parts = []
    for idx, (si, ci) in enumerate(seq):
        u, v = uv
        if idx + 1 < len(seq):
            uv = up(*seq[idx + 1])
        for s in range(nslab):
            slab = ci * nslab + s
            cl = ci * fc + s * LANES
            us = u[:, s * LANES:(s + 1) * LANES]
            buf = ubuf_ref.at[idx % 2, s]
            buf[0:SUBLANES, :] = tail_ref[slab]
            buf[SUBLANES:SUBLANES + sub, :] = us
            u1 = buf[SUBLANES - 1:SUBLANES - 1 + sub, :]
            u2 = buf[SUBLANES - 2:SUBLANES - 2 + sub, :]
            tail_ref[slab] = us[sub - SUBLANES:]
            cw = cw_ref[:, cl:cl + LANES]
            uc = cb_ref[:, cl:cl + LANES] + u2 * cw[0:1] + u1 * cw[1:2] + us * cw[2:3]
            parts.append((_silu(uc) * v[:, s * LANES:(s + 1) * LANES]).astype(BF16))
        last = ci == nchunk - 1
        if (ci + 1) % group == 0 or last:
            g0 = (ci // group) * group
            part = _dot(jnp.concatenate(parts, axis=1), wdn_ref[g0 * fc:(ci + 1) * fc, :])
            parts = []
            rows = slice(si * sub, (si + 1) * sub)
            if g0 == 0:
                acc_ref[rows, :] = part
            else:
                acc_ref[rows, :] += part
            if last:
                o_ref[0, rows, :] = _layer_norm(
                    alpha * x_ref[0, rows, :] + mod[5:6] * acc_ref[rows, :], lng_ref[...], lnb_ref[...])


def _conv_ffn(x, mod_all, w_up, conv_w, conv_b, w_down, ln_g, ln_b, *, layer, alpha, tm, sub,
              fc=256, group=4):
    bsz, t, d = x.shape
    dff = w_down.shape[1]
    assert dff % fc == 0 and fc % LANES == 0 and CONV_WIDTH == 3 and tm % sub == 0 and sub >= SUBLANES
    return pl.pallas_call(
        functools.partial(_ffn_kernel, alpha=alpha, tm=tm, sub=sub, fc=fc, dff=dff, group=group),
        grid=(bsz, t // tm),
        in_specs=[
            _row_spec(tm, d),
            _mod_spec(layer, d),
            _layer_spec((d, 2 * dff), layer),
            _layer_spec((CONV_WIDTH, dff), layer),
            _layer_spec((1, dff), layer),
            _layer_spec((dff, d), layer),
            _layer_spec((1, d), 2 * layer + 1),
            _layer_spec((1, d), 2 * layer + 1),
        ],
        out_specs=_row_spec(tm, d),
        out_shape=jax.ShapeDtypeStruct((bsz, t, d), F32),
        scratch_shapes=[pltpu.VMEM((dff // LANES, SUBLANES, LANES), F32),
                        pltpu.VMEM((2, fc // LANES, sub + SUBLANES, LANES), F32),
                        pltpu.VMEM((tm, d), F32)],
        compiler_params=_cparams(2),
        name="ffn_ln",
    )(x, mod_all, w_up, conv_w, conv_b, w_down, ln_g, ln_b)


def _pick_tile(t, pref):
    tm = min(t, pref)
    assert t % tm == 0
    return tm


def kernel(x, c, positions, ada_w, ada_b, ln_g, ln_b, ab_w_in, ab_w_out, hgrn_lb_raw, hgrn_norm_g,
           attn_sinks, pool_w, pool_scale, ffn_w_up, ffn_conv_w, ffn_conv_b, ffn_w_down):
    bsz, t, d = x.shape
    depth = ada_w.shape[0]
    alpha = (2.0 * depth) ** 0.25
    tm = _pick_tile(t, ROW_TILE)
    tq = _pick_tile(t, EVEN_TILE)

    mod_all = _ada_mod(c, ada_w, ada_b).reshape(depth, bsz, 6, d)
    cos_t, sin_t = _rope_tables(positions, _pick_tile(t, ROW_TILE))
    w_in = ab_w_in.astype(BF16)
    w_out = ab_w_out.astype(BF16)
    w_pool = pool_w.astype(BF16)
    w_up = ffn_w_up.astype(BF16)
    w_down = ffn_w_down.astype(BF16)
    ln_g2 = ln_g.reshape(2 * depth, 1, d)
    ln_b2 = ln_b.reshape(2 * depth, 1, d)
    norm_g = hgrn_norm_g.reshape(-1, 1, A_DV)
    pool_sc = pool_scale.reshape(-1, 1, d)
    conv_b = ffn_conv_b.reshape(depth, 1, -1)

    for l in range(depth):
        if l % 2 == 0:
            x = _even_layer(x, mod_all, w_in, w_out, hgrn_lb_raw, norm_g, cos_t, sin_t, attn_sinks,
                            ln_g2, ln_b2, layer=l, alpha=alpha, tq=tq)
        else:
            x = _pool_mixer(x, mod_all, w_pool, pool_sc, ln_g2, ln_b2, layer=l, alpha=alpha, tm=tm)
        x = _conv_ffn(x, mod_all, w_up, ffn_conv_w, conv_b, w_down, ln_g2, ln_b2,
                      layer=l, alpha=alpha, tm=_pick_tile(t, FFN_TILE), sub=tm)
    return x
```

```python
import functools

import numpy as np
import jax
import jax.numpy as jnp
from jax import lax
from jax.experimental import pallas as pl
from jax.experimental.pallas import tpu as pltpu

F32 = jnp.float32
BF16 = jnp.bfloat16

A_HEADS = 4
A_DK = 128
A_DV = 128
A_KD = A_HEADS * A_DK
A_WIDTH = A_HEADS * A_DV
A_CHUNK = 64
B_Q_HEADS = 8
B_KV_HEADS = 2
B_HEAD_DIM = 64
B_WIDTH = B_Q_HEADS * B_HEAD_DIM
B_KV_WIDTH = B_KV_HEADS * B_HEAD_DIM
WINDOW = 128
ROPE_DIM = B_HEAD_DIM // 4
ROPE_THETA = 500000.0
POOL_WINDOWS = (2, 4, 8, 16)
POOL_HALO = 16
CONV_WIDTH = 3
LN_EPS = 1e-5
RMS_EPS = 1e-6

LANES = 128
SUBLANES = 8
VMEM_LIMIT_BYTES = 56 * 1024 * 1024

EVEN_TILE = 512
ROW_TILE = 512
FFN_TILE = 1024

NEG_BIG = -1e30
HGRN_FACTORED_MAX_DECAY = 85.0


def _cparams(n_axes):
    return pltpu.CompilerParams(
        dimension_semantics=("arbitrary",) * n_axes,
        vmem_limit_bytes=VMEM_LIMIT_BYTES,
    )


def _dot(a, b):
    return jnp.dot(a, b, preferred_element_type=F32)


def _dot_nt(a, b):
    return lax.dot_general(a, b, (((1,), (1,)), ((), ())), preferred_element_type=F32)


def _dot_tn(a, b):
    return lax.dot_general(a, b, (((0,), (0,)), ((), ())), preferred_element_type=F32)


def _layer_norm(r, g, b):
    mu = jnp.mean(r, axis=-1, keepdims=True)
    d = r - mu
    var = jnp.mean(d * d, axis=-1, keepdims=True)
    return d * lax.rsqrt(var + LN_EPS) * g + b


def _sigmoid_pair(z):
    e = jnp.exp(-jnp.abs(z))
    r = 1.0 / (1.0 + e)
    er = e * r
    pos = z >= 0
    return jnp.where(pos, r, er), jnp.where(pos, er, r)


def _silu(z):
    return z * (0.5 * jnp.tanh(0.5 * z) + 0.5)


def _row_spec(tm, width):
    return pl.BlockSpec((1, tm, width), lambda b, i: (b, i, 0))


def _layer_spec(shape, layer):
    zeros = (0,) * len(shape)
    return pl.BlockSpec((None,) + tuple(shape), lambda b, i: (layer,) + zeros)


def _mod_spec(layer, d):
    return pl.BlockSpec((None, 1, 6, d), lambda b, i: (layer, b, 0, 0))


def _ada_kernel(c_ref, w_ref, b_ref, o_ref):
    ca = _silu(c_ref[...]).astype(BF16)
    o_ref[0] = _dot(ca, w_ref[0].astype(BF16)) + b_ref[0]


def _ada_mod(c, ada_w, ada_b, tn=1536):
    depth, d, n = ada_w.shape
    bsz = c.shape[0]
    assert n % tn == 0
    return pl.pallas_call(
        _ada_kernel,
        grid=(depth, n // tn),
        in_specs=[
            pl.BlockSpec((bsz, d), lambda l, j: (0, 0)),
            pl.BlockSpec((1, d, tn), lambda l, j: (l, 0, j)),
            pl.BlockSpec((1, 1, tn), lambda l, j: (l, 0, j)),
        ],
        out_specs=pl.BlockSpec((1, bsz, tn), lambda l, j: (l, 0, j)),
        out_shape=jax.ShapeDtypeStruct((depth, bsz, n), F32),
        compiler_params=_cparams(2),
        name="ada_mod",
    )(c, ada_w, ada_b.reshape(depth, 1, n))


def _rope_lane_table():
    inv = ROPE_THETA ** (-np.arange(0, ROPE_DIM, 2, dtype=np.float64) / ROPE_DIM)
    half = ROPE_DIM // 2
    tab = np.zeros((1, LANES), np.float32)
    for l in range(LANES):
        dpos = l % B_HEAD_DIM
        if dpos < ROPE_DIM:
            tab[0, l] = inv[dpos % half]
    return tab


def _rope_kernel(pos_ref, invf_ref, cos_ref, sin_ref):
    half = ROPE_DIM // 2
    ang = pos_ref[0] * invf_ref[...]
    dpos = lax.broadcasted_iota(jnp.int32, ang.shape, 1) & (B_HEAD_DIM - 1)
    sinv = jnp.sin(ang)
    cos_ref[0] = jnp.cos(ang)
    sin_ref[0] = jnp.where(dpos < half, -sinv, jnp.where(dpos < ROPE_DIM, sinv, 0.0))


def _rope_tables(positions, tm):
    bsz, t = positions.shape
    pos_b = jnp.broadcast_to(positions.astype(F32)[:, :, None], (bsz, t, LANES))
    spec = _row_spec(tm, LANES)
    return pl.pallas_call(
        _rope_kernel,
        grid=(bsz, t // tm),
        in_specs=[spec, pl.BlockSpec((1, LANES), lambda b, i: (0, 0))],
        out_specs=[spec, spec],
        out_shape=[jax.ShapeDtypeStruct((bsz, t, LANES), F32)] * 2,
        compiler_params=_cparams(2),
        name="rope_tables",
    )(pos_b, jnp.asarray(_rope_lane_table()))


def _hgrn_prepare(za_ref, lbraw_ref, k_ref, bl_ref, *, layer_e, tq):
    lbraw = lbraw_ref[...]
    lmax = jnp.max(lbraw, axis=0, keepdims=True)
    lexp = jnp.exp(lbraw - lmax)
    lbp = lexp / jnp.sum(lexp, axis=0, keepdims=True)
    lb_all = jnp.zeros((1, A_KD), F32)
    for i in range(1, layer_e + 1):
        lb_all = lb_all + lbp[i:i + 1]

    hc = A_CHUNK // 2
    mid = hc // 2 - 1
    rh_t = lax.broadcasted_iota(jnp.int32, (tq, A_DK), 0) & (hc - 1)
    worst = jnp.zeros((1, A_DK), F32)
    for hd in range(A_HEADS):
        lo = hd * A_DK
        lb = lb_all[:, lo:lo + A_DK]
        sp, sn = _sigmoid_pair(za_ref[:, A_KD + lo:A_KD + lo + A_DK])
        k_ref[hd] = (1.0 - lb) * sn
        bl = jnp.log(lb + (1.0 - lb) * sp)
        s = 1
        while s < hc:
            bl = bl + jnp.where(rh_t >= s, pltpu.roll(bl, s, 0), 0.0)
            s *= 2
        bl_ref[hd] = bl
        for r0 in range(0, tq, hc):
            b_mid = bl[r0 + mid:r0 + mid + 1]
            worst = jnp.maximum(worst, jnp.maximum(-b_mid, b_mid - bl[r0 + hc - 1:r0 + hc]))
    return jnp.max(worst) <= HGRN_FACTORED_MAX_DECAY


def _hgrn_chunks(factored, za_ref, ng_ref, st_ref, k_ref, bl_ref, out_ref, *, tq):
    c = A_CHUNK
    hc = c // 2
    mid = hc // 2 - 1
    rows = lax.broadcasted_iota(jnp.int32, (c, A_DK), 0)
    upper_half = rows >= hc
    sub = rows & (SUBLANES - 1)
    ri = lax.broadcasted_iota(jnp.int32, (c, c), 0)
    ci = lax.broadcasted_iota(jnp.int32, (c, c), 1)
    ng = ng_ref[...]

    def inputs(ch, hd):
        r0 = ch * c
        qc = za_ref[r0:r0 + c, hd * A_DK:(hd + 1) * A_DK]
        vc = za_ref[r0:r0 + c, 2 * A_KD + hd * A_DV:2 * A_KD + (hd + 1) * A_DV]
        return qc, vc, k_ref[hd, r0:r0 + c], bl_ref[hd, r0:r0 + c]

    def emit(ch, hd, o):
        r0 = ch * c
        ag = za_ref[r0:r0 + c, 2 * A_KD + A_WIDTH + hd * A_DV:2 * A_KD + A_WIDTH + (hd + 1) * A_DV]
        o = o * lax.rsqrt(jnp.mean(o * o, axis=-1, keepdims=True) + RMS_EPS) * ng
        out_ref[r0:r0 + c, hd * A_DV:(hd + 1) * A_DV] = (o * _silu(ag)).astype(out_ref.dtype)

    units = [(ch, hd) for ch in range(tq // c) for hd in range(A_HEADS)]

    if factored:
        for ch, hd in units:
            qc, vc, kc, bl = inputs(ch, hd)
            st = st_ref[hd]
            m0, t0 = bl[mid:mid + 1], bl[hc - 1:hc]
            m1, t1 = bl[hc + mid:hc + mid + 1], bl[c - 1:c]
            bm = bl - jnp.where(upper_half, m1, m0)
            qd = qc * jnp.exp(bm)
            kd = kc * jnp.exp(-bm)
            qdb = qd.astype(BF16)
            vb = vc.astype(BF16)
            a_in = _dot_nt(qdb, kd.astype(BF16))
            a_x = _dot_nt(qdb, (kd * jnp.exp(m1 + t0 - m0)).astype(BF16))
            att = jnp.where(((ri ^ ci) < hc) & (ci <= ri), a_in,
                            jnp.where((ri >= hc) & (ci < hc), a_x, 0.0))
            qe = qd * jnp.where(upper_half, jnp.exp(m1 + t0), jnp.exp(m0))
            o = _dot_nt(qe.astype(BF16), st.astype(BF16)) + _dot(att.astype(BF16), vb)
            ke = (kd * jnp.where(upper_half, jnp.exp(t1 - m1), jnp.exp(t0 + t1 - m0))).astype(BF16)
            st_ref[hd] = jnp.exp(t0 + t1) * st + _dot_tn(vb, ke)
            emit(ch, hd, o)
        return

    for ch, hd in units:
        qc, vc, kc, bl = inputs(ch, hd)
        st = st_ref[hd]
        vb = vc.astype(BF16)
        bc = bl + jnp.where(upper_half, bl[hc - 1:hc], 0.0)
        b_last = bc[c - 1:c]
        o = _dot_nt((qc * jnp.exp(bc)).astype(BF16), st.astype(BF16))
        att = jnp.zeros((c, c), F32)
        m = hc
        while m >= SUBLANES:
            bref = jnp.concatenate(
                [jnp.broadcast_to(bc[2 * m * j + m - 1:2 * m * j + m], (2 * m, A_DK))
                 for j in range(c // (2 * m))], axis=0)
            upper = (rows & m) != 0
            qm = jnp.where(upper, qc * jnp.exp(jnp.minimum(bc - bref, 0.0)), 0.0)
            km = jnp.where(upper, 0.0, kc * jnp.exp(jnp.minimum(bref - bc, 0.0)))
            a = _dot_nt(qm.astype(BF16), km.astype(BF16))
            att = att + jnp.where((ri ^ ci) < 2 * m, a, 0.0)
            m //= 2
        o = o + _dot(att.astype(BF16), vb)
        for dd in range(SUBLANES):
            if dd == 0:
                pd = qc * kc
                vs = vc
            else:
                ks = pltpu.roll(kc, dd, 0)
                bs = pltpu.roll(bc, dd, 0)
                vs = pltpu.roll(vc, dd, 0)
                pd = qc * ks * jnp.exp(jnp.minimum(bc - bs, 0.0))
                pd = jnp.where(sub >= dd, pd, 0.0)
            o = o + jnp.sum(pd, axis=-1, keepdims=True) * vs
        ke = (kc * jnp.exp(b_last - bc)).astype(BF16)
        st_ref[hd] = jnp.exp(b_last) * st + _dot_tn(vb, ke)
        emit(ch, hd, o)


def _swa_tile(zb_ref, cos_ref, sin_ref, sink_ref, kp_ref, vp_ref, out_ref, *, layer_e, ti, tq, col0):
    w = WINDOW
    half = ROPE_DIM // 2
    hd = B_HEAD_DIM
    ngrp = B_WIDTH // LANES
    assert 2 * hd == LANES and B_KV_WIDTH == LANES and (B_Q_HEADS // B_KV_HEADS) == 4

    cosv = cos_ref[0]
    sinv = sin_ref[0]
    lane = lax.broadcasted_iota(jnp.int32, (tq, LANES), 1)
    first_half = (lane & (hd - 1)) < half

    def rope(xx):
        partner = jnp.where(first_half, pltpu.roll(xx, LANES - half, 1), pltpu.roll(xx, half, 1))
        return xx * cosv + partner * sinv

    k_new = rope(zb_ref[:, B_WIDTH:B_WIDTH + B_KV_WIDTH])
    v_new = zb_ref[:, B_WIDTH + B_KV_WIDTH:B_WIDTH + 2 * B_KV_WIDTH]
    kext = jnp.concatenate([kp_ref[...], k_new], axis=0)
    vext = jnp.concatenate([vp_ref[...], v_new], axis=0)
    kp_ref[...] = k_new[tq - w:]
    vp_ref[...] = v_new[tq - w:]

    lo_lanes = lax.broadcasted_iota(jnp.int32, kext.shape, 1) < hd

    def lane_variants(a):
        sw = pltpu.roll(a, hd, 1)
        return [[jnp.where(lo_lanes, a, 0.0).astype(BF16), jnp.where(lo_lanes, 0.0, sw).astype(BF16)],
                [jnp.where(lo_lanes, sw, 0.0).astype(BF16), jnp.where(lo_lanes, 0.0, a).astype(BF16)]]

    kvar = lane_variants(kext)
    vvar = lane_variants(vext)

    qs = [(rope(zb_ref[:, j * LANES:(j + 1) * LANES]) * (hd ** -0.5)).astype(BF16)
          for j in range(ngrp)]

    qi = (lax.broadcasted_iota(jnp.int32, (2 * w, 2 * w), 0) & (w - 1)) + w
    ki = lax.broadcasted_iota(jnp.int32, (2 * w, 2 * w), 1)
    rel = qi - ki
    allowed = (rel >= 0) & (rel < w)
    bias = jnp.where(allowed, 0.0, NEG_BIG)
    bias_first = jnp.where(allowed & ((ti > 0) | (ki >= w)), 0.0, NEG_BIG)
    top = lax.broadcasted_iota(jnp.int32, (2 * w, 1), 0) < w
    lo_out = lax.broadcasted_iota(jnp.int32, (2 * w, LANES), 1) < hd
    lo_rows = lax.broadcasted_iota(jnp.int32, (4 * w, LANES), 0) < 2 * w
    lo_cols = lax.broadcasted_iota(jnp.int32, (4 * w, LANES), 1) < hd
    ones_sel = jnp.where(lo_rows == lo_cols, 1.0, 0.0).astype(BF16)

    units = [(j, hk) for j in range(tq // w) for hk in range(B_KV_HEADS)]
    stage1 = []
    for j, hk in units:
        bj = bias_first if j == 0 else bias
        qa = jnp.concatenate([qs[2 * hk][j * w:(j + 1) * w], qs[2 * hk + 1][j * w:(j + 1) * w]],
                             axis=0)
        for par in range(2):
            sink = jnp.where(top, sink_ref[layer_e, 4 * hk + par], sink_ref[layer_e, 4 * hk + 2 + par])
            s = _dot_nt(qa, kvar[hk][par][j * w:(j + 2) * w]) + bj
            mx = jnp.maximum(jnp.max(s, axis=-1, keepdims=True), sink)
            stage1.append((s, mx, sink))
    yield
    stage2 = []
    for s, mx, sink in stage1:
        stage2.append((jnp.exp(s - mx).astype(BF16), jnp.exp(sink - mx)))
    yield
    for ui, (j, hk) in enumerate(units):
        (p0, e0), (p1, e1) = stage2[2 * ui], stage2[2 * ui + 1]
        pp = jnp.concatenate([p0, p1], axis=1)
        vv = jnp.concatenate([vvar[hk][0][j * w:(j + 2) * w], vvar[hk][1][j * w:(j + 2) * w]],
                             axis=0)
        od = _dot(pp, jnp.concatenate([vv, ones_sel], axis=1))
        den = od[:, LANES:] + jnp.where(lo_out, e0, e1)
        o = (od[:, :LANES] / den).astype(out_ref.dtype)
        c0 = col0 + 2 * hk * LANES
        out_ref[j * w:(j + 1) * w, c0:c0 + LANES] = o[:w]
        out_ref[j * w:(j + 1) * w, c0 + LANES:c0 + 2 * LANES] = o[w:]


def _even_kernel(x_ref, mod_ref, win_ref, wout_ref, lbraw_ref, ng_ref, cos_ref, sin_ref, sink_ref,
                 lng_ref, lnb_ref, o_ref,
                 za_ref, zb_ref, oc_ref, st_ref, k_ref, bl_ref, kp_ref, vp_ref,
                 *, layer_e, tq, alpha):
    ti = pl.program_id(1)
    na = 2 * A_KD + 2 * A_WIDTH

    @pl.when(ti == 0)
    def _():
        st_ref[...] = jnp.zeros_like(st_ref)
        kp_ref[...] = jnp.zeros_like(kp_ref)
        vp_ref[...] = jnp.zeros_like(vp_ref)

    mod = mod_ref[0]
    h = (x_ref[0] * (1.0 + mod[1:2]) + mod[0:1]).astype(BF16)

    def project(lo, width, dst_ref, dst_lo):
        dst_ref[:, dst_lo:dst_lo + width] = _dot(h, win_ref[:, lo:lo + width])

    project(na, B_WIDTH + 2 * B_KV_WIDTH, zb_ref, 0)
    project(A_KD, A_KD, za_ref, A_KD)
    swa = _swa_tile(zb_ref, cos_ref, sin_ref, sink_ref, kp_ref, vp_ref, oc_ref,
                    layer_e=layer_e, ti=ti, tq=tq, col0=A_WIDTH)
    next(swa)
    project(0, A_KD, za_ref, 0)
    project(2 * A_KD, A_WIDTH, za_ref, 2 * A_KD)
    factored_ok = _hgrn_prepare(za_ref, lbraw_ref, k_ref, bl_ref, layer_e=layer_e, tq=tq)
    next(swa)
    project(2 * A_KD + A_WIDTH, A_WIDTH, za_ref, 2 * A_KD + A_WIDTH)
    for _ in swa:
        pass

    def finish(factored):
        y = _dot(oc_ref[:, A_WIDTH:], wout_ref[A_WIDTH:, :])
        _hgrn_chunks(factored, za_ref, ng_ref, st_ref, k_ref, bl_ref, oc_ref, tq=tq)
        y = y + _dot(oc_ref[:, :A_WIDTH], wout_ref[:A_WIDTH, :])
        o_ref[0] = _layer_norm(alpha * x_ref[0] + mod[2:3] * y, lng_ref[...], lnb_ref[...])

    pl.when(factored_ok)(functools.partial(finish, True))
    pl.when(jnp.logical_not(factored_ok))(functools.partial(finish, False))


def _even_layer(x, mod_all, w_in, w_out, lb_raw, norm_g, cos_t, sin_t, sinks, ln_g, ln_b,
                *, layer, alpha, tq):
    bsz, t, d = x.shape
    e = layer // 2
    n_in = w_in.shape[-1]
    w = WINDOW
    assert tq % w == 0 and tq % A_CHUNK == 0
    const2 = lambda b, i: (0, 0)
    return pl.pallas_call(
        functools.partial(_even_kernel, layer_e=e, tq=tq, alpha=alpha),
        grid=(bsz, t // tq),
        in_specs=[
            _row_spec(tq, d),
            _mod_spec(layer, d),
            _layer_spec((d, n_in), e),
            _layer_spec((A_WIDTH + B_WIDTH, d), e),
            pl.BlockSpec(lb_raw.shape, const2),
            _layer_spec((1, A_DV), e),
            _row_spec(tq, LANES),
            _row_spec(tq, LANES),
            pl.BlockSpec(memory_space=pltpu.SMEM),
            _layer_spec((1, d), 2 * layer),
            _layer_spec((1, d), 2 * layer),
        ],
        out_specs=_row_spec(tq, d),
        out_shape=jax.ShapeDtypeStruct((bsz, t, d), F32),
        scratch_shapes=[
            pltpu.VMEM((tq, 2 * A_KD + 2 * A_WIDTH), F32),
            pltpu.VMEM((tq, B_WIDTH + 2 * B_KV_WIDTH), F32),
            pltpu.VMEM((tq, A_WIDTH + B_WIDTH), BF16),
            pltpu.VMEM((A_HEADS, A_DV, A_DK), F32),
            pltpu.VMEM((A_HEADS, tq, A_DK), F32),
            pltpu.VMEM((A_HEADS, tq, A_DK), F32),
            pltpu.VMEM((w, B_KV_WIDTH), F32),
            pltpu.VMEM((w, B_KV_WIDTH), F32),
        ],
        compiler_params=_cparams(2),
        name="even_layer",
    )(x, mod_all, w_in, w_out, lb_raw, norm_g, cos_t, sin_t, sinks, ln_g, ln_b)


def _pool_kernel(x_ref, mod_ref, w_ref, scale_ref, lng_ref, lnb_ref, o_ref, carry_ref, y_ref,
                 *, alpha, tm):
    ti = pl.program_id(1)

    @pl.when(ti == 0)
    def _():
        carry_ref[...] = jnp.zeros_like(carry_ref)

    x = x_ref[0]
    mod = mod_ref[0]
    h = x * (1.0 + mod[1:2]) + mod[0:1]
    ext = jnp.concatenate([carry_ref[...], h], axis=0)
    carry_ref[...] = h[tm - POOL_HALO:]

    t_glob = ti * tm + lax.broadcasted_iota(jnp.int32, (tm, 1), 0)
    gw = x.shape[-1] // len(POOL_WINDOWS)
    for gi, win in enumerate(POOL_WINDOWS):
        e = ext[:, gi * gw:(gi + 1) * gw]
        s = 1
        while s < win:
            e = e + pltpu.roll(e, s, 0)
            s *= 2
        inv_cnt = 1.0 / jnp.minimum(t_glob + 1, win).astype(F32)
        pooled = e[POOL_HALO:] * inv_cnt - h[:, gi * gw:(gi + 1) * gw]
        y_ref[:, gi * gw:(gi + 1) * gw] = _dot(pooled.astype(BF16), w_ref[gi])
    y = y_ref[...] * scale_ref[...]
    o_ref[0] = _layer_norm(alpha * x + mod[2:3] * y, lng_ref[...], lnb_ref[...])


def _pool_mixer(x, mod_all, w_grp, scale, ln_g, ln_b, *, layer, alpha, tm):
    bsz, t, d = x.shape
    o = layer // 2
    _, ng, gw, _ = w_grp.shape
    assert POOL_HALO >= max(POOL_WINDOWS) and tm >= POOL_HALO and gw % LANES == 0
    assert all(w & (w - 1) == 0 and w >= 2 for w in POOL_WINDOWS)
    return pl.pallas_call(
        functools.partial(_pool_kernel, alpha=alpha, tm=tm),
        grid=(bsz, t // tm),
        in_specs=[
            _row_spec(tm, d),
            _mod_spec(layer, d),
            _layer_spec((ng, gw, gw), o),
            _layer_spec((1, d), o),
            _layer_spec((1, d), 2 * layer),
            _layer_spec((1, d), 2 * layer),
        ],
        out_specs=_row_spec(tm, d),
        out_shape=jax.ShapeDtypeStruct((bsz, t, d), F32),
        scratch_shapes=[pltpu.VMEM((POOL_HALO, d), F32), pltpu.VMEM((tm, d), F32)],
        compiler_params=_cparams(2),
        name="odd_pool_ln",
    )(x, mod_all, w_grp, scale, ln_g, ln_b)


def _ffn_kernel(x_ref, mod_ref, wup_ref, cw_ref, cb_ref, wdn_ref, lng_ref, lnb_ref, o_ref,
                tail_ref, ubuf_ref, acc_ref, *, alpha, tm, sub, fc, dff, group):
    ti = pl.program_id(1)
    nslab = fc // LANES
    nchunk = dff // fc
    nsub = tm // sub

    @pl.when(ti == 0)
    def _():
        tail_ref[...] = jnp.zeros_like(tail_ref)

    mod = mod_ref[0]
    scale = 1.0 + mod[4:5]
    hs = {}

    def up(si, ci):
        if si not in hs:
            hs[si] = (x_ref[0, si * sub:(si + 1) * sub, :] * scale + mod[3:4]).astype(BF16)
        lo = ci * fc
        return (_dot(hs[si], wup_ref[:, lo:lo + fc]),
                _dot(hs[si], wup_ref[:, dff + lo:dff + lo + fc]))

    seq = [(si, ci) for si in range(nsub) for ci in range(nchunk)]
    uv = up(*seq[0])
    parts = []
    for idx, (si, ci) in enumerate(seq):
        u, v = uv
        if idx + 1 < len(seq):
            uv = up(*seq[idx + 1])
        for s in range(nslab):
            slab = ci * nslab + s
            cl = ci * fc + s * LANES
            us = u[:, s * LANES:(s + 1) * LANES]
            buf = ubuf_ref.at[idx % 2, s]
            buf[0:SUBLANES, :] = tail_ref[slab]
            buf[SUBLANES:SUBLANES + sub, :] = us
            u1 = buf[SUBLANES - 1:SUBLANES - 1 + sub, :]
            u2 = buf[SUBLANES - 2:SUBLANES - 2 + sub, :]
            tail_ref[slab] = us[sub - SUBLANES:]
            cw = cw_ref[:, cl:cl + LANES]
            uc = cb_ref[:, cl:cl + LANES] + u2 * cw[0:1] + u1 * cw[1:2] + us * cw[2:3]
            parts.append((_silu(uc) * v[:, s * LANES:(s + 1) * LANES]).astype(BF16))
        last = ci == nchunk - 1
        if (ci + 1) % group == 0 or last:
            g0 = (ci // group) * group
            part = _dot(jnp.concatenate(parts, axis=1), wdn_ref[g0 * fc:(ci + 1) * fc, :])
            parts = []
            rows = slice(si * sub, (si + 1) * sub)
            if g0 == 0:
                acc_ref[rows, :] = part
            else:
                acc_ref[rows, :] += part
            if last:
                o_ref[0, rows, :] = _layer_norm(
                    alpha * x_ref[0, rows, :] + mod[5:6] * acc_ref[rows, :], lng_ref[...], lnb_ref[...])


def _conv_ffn(x, mod_all, w_up, conv_w, conv_b, w_down, ln_g, ln_b, *, layer, alpha, tm, sub,
              fc=256, group=None):
    bsz, t, d = x.shape
    dff = w_down.shape[1]
    group = group or dff // fc
    assert dff % fc == 0 and fc % LANES == 0 and CONV_WIDTH == 3 and tm % sub == 0 and sub >= SUBLANES
    return pl.pallas_call(
        functools.partial(_ffn_kernel, alpha=alpha, tm=tm, sub=sub, fc=fc, dff=dff, group=group),
        grid=(bsz, t // tm),
        in_specs=[
            _row_spec(tm, d),
            _mod_spec(layer, d),
            _layer_spec((d, 2 * dff), layer),
            _layer_spec((CONV_WIDTH, dff), layer),
            _layer_spec((1, dff), layer),
            _layer_spec((dff, d), layer),
            _layer_spec((1, d), 2 * layer + 1),
            _layer_spec((1, d), 2 * layer + 1),
        ],
        out_specs=_row_spec(tm, d),
        out_shape=jax.ShapeDtypeStruct((bsz, t, d), F32),
        scratch_shapes=[pltpu.VMEM((dff // LANES, SUBLANES, LANES), F32),
                        pltpu.VMEM((2, fc // LANES, sub + SUBLANES, LANES), F32),
                        pltpu.VMEM((tm, d), F32)],
        compiler_params=_cparams(2),
        name="ffn_ln",
    )(x, mod_all, w_up, conv_w, conv_b, w_down, ln_g, ln_b)


def _pick_tile(t, pref):
    tm = min(t, pref)
    assert t % tm == 0
    return tm


def kernel(x, c, positions, ada_w, ada_b, ln_g, ln_b, ab_w_in, ab_w_out, hgrn_lb_raw, hgrn_norm_g,
           attn_sinks, pool_w, pool_scale, ffn_w_up, ffn_conv_w, ffn_conv_b, ffn_w_down):
    bsz, t, d = x.shape
    depth = ada_w.shape[0]
    alpha = (2.0 * depth) ** 0.25
    tm = _pick_tile(t, ROW_TILE)
    tq = _pick_tile(t, EVEN_TILE)

    mod_all = _ada_mod(c, ada_w, ada_b).reshape(depth, bsz, 6, d)
    cos_t, sin_t = _rope_tables(positions, _pick_tile(t, ROW_TILE))
    w_in = ab_w_in.astype(BF16)
    w_out = ab_w_out.astype(BF16)
    w_pool = pool_w.astype(BF16)
    w_up = ffn_w_up.astype(BF16)
    w_down = ffn_w_down.astype(BF16)
    ln_g2 = ln_g.reshape(2 * depth, 1, d)
    ln_b2 = ln_b.reshape(2 * depth, 1, d)
    norm_g = hgrn_norm_g.reshape(-1, 1, A_DV)
    pool_sc = pool_scale.reshape(-1, 1, d)
    conv_b = ffn_conv_b.reshape(depth, 1, -1)

    for l in range(depth):
        if l % 2 == 0:
            x = _even_layer(x, mod_all, w_in, w_out, hgrn_lb_raw, norm_g, cos_t, sin_t, attn_sinks,
                            ln_g2, ln_b2, layer=l, alpha=alpha, tq=tq)
        else:
            x = _pool_mixer(x, mod_all, w_pool, pool_sc, ln_g2, ln_b2, layer=l, alpha=alpha, tm=tm)
        x = _conv_ffn(x, mod_all, w_up, ffn_conv_w, conv_b, w_down, ln_g2, ln_b2,
                      layer=l, alpha=alpha, tm=_pick_tile(t, FFN_TILE), sub=tm)
    return x
```

```python
import functools

import numpy as np
import jax
import jax.numpy as jnp
from jax import lax
from jax.experimental import pallas as pl
from jax.experimental.pallas import tpu as pltpu

F32 = jnp.float32
BF16 = jnp.bfloat16

A_HEADS = 4
A_DK = 128
A_DV = 128
A_KD = A_HEADS * A_DK
A_WIDTH = A_HEADS * A_DV
A_CHUNK = 64
B_Q_HEADS = 8
B_KV_HEADS = 2
B_HEAD_DIM = 64
B_WIDTH = B_Q_HEADS * B_HEAD_DIM
B_KV_WIDTH = B_KV_HEADS * B_HEAD_DIM
WINDOW = 128
ROPE_DIM = B_HEAD_DIM // 4
ROPE_THETA = 500000.0
POOL_WINDOWS = (2, 4, 8, 16)
POOL_HALO = 16
CONV_WIDTH = 3
LN_EPS = 1e-5
RMS_EPS = 1e-6

LANES = 128
SUBLANES = 8
VMEM_LIMIT_BYTES = 56 * 1024 * 1024

EVEN_TILE = 512
ROW_TILE = 512
FFN_TILE = 1024
FFN_SUB = 256

NEG_BIG = -1e30
HGRN_FACTORED_MAX_DECAY = 85.0


def _cparams(n_axes):
    return pltpu.CompilerParams(
        dimension_semantics=("arbitrary",) * n_axes,
        vmem_limit_bytes=VMEM_LIMIT_BYTES,
    )


def _dot(a, b):
    return jnp.dot(a, b, preferred_element_type=F32)


def _dot_nt(a, b):
    return lax.dot_general(a, b, (((1,), (1,)), ((), ())), preferred_element_type=F32)


def _dot_tn(a, b):
    return lax.dot_general(a, b, (((0,), (0,)), ((), ())), preferred_element_type=F32)


def _layer_norm(r, g, b):
    mu = jnp.mean(r, axis=-1, keepdims=True)
    d = r - mu
    var = jnp.mean(d * d, axis=-1, keepdims=True)
    return d * lax.rsqrt(var + LN_EPS) * g + b


def _sigmoid_pair(z):
    e = jnp.exp(-jnp.abs(z))
    r = 1.0 / (1.0 + e)
    er = e * r
    pos = z >= 0
    return jnp.where(pos, r, er), jnp.where(pos, er, r)


def _silu(z):
    return z * (0.5 * jnp.tanh(0.5 * z) + 0.5)


def _row_spec(tm, width):
    return pl.BlockSpec((1, tm, width), lambda b, i: (b, i, 0))


def _layer_spec(shape, layer):
    zeros = (0,) * len(shape)
    return pl.BlockSpec((None,) + tuple(shape), lambda b, i: (layer,) + zeros)


def _mod_spec(layer, d):
    return pl.BlockSpec((None, 1, 6, d), lambda b, i: (layer, b, 0, 0))


def _ada_kernel(c_ref, w_ref, b_ref, o_ref):
    ca = _silu(c_ref[...]).astype(BF16)
    o_ref[0] = _dot(ca, w_ref[0].astype(BF16)) + b_ref[0]


def _ada_mod(c, ada_w, ada_b, tn=1536):
    depth, d, n = ada_w.shape
    bsz = c.shape[0]
    assert n % tn == 0
    return pl.pallas_call(
        _ada_kernel,
        grid=(depth, n // tn),
        in_specs=[
            pl.BlockSpec((bsz, d), lambda l, j: (0, 0)),
            pl.BlockSpec((1, d, tn), lambda l, j: (l, 0, j)),
            pl.BlockSpec((1, 1, tn), lambda l, j: (l, 0, j)),
        ],
        out_specs=pl.BlockSpec((1, bsz, tn), lambda l, j: (l, 0, j)),
        out_shape=jax.ShapeDtypeStruct((depth, bsz, n), F32),
        compiler_params=_cparams(2),
        name="ada_mod",
    )(c, ada_w, ada_b.reshape(depth, 1, n))


def _rope_lane_table():
    inv = ROPE_THETA ** (-np.arange(0, ROPE_DIM, 2, dtype=np.float64) / ROPE_DIM)
    half = ROPE_DIM // 2
    tab = np.zeros((1, LANES), np.float32)
    for l in range(LANES):
        dpos = l % B_HEAD_DIM
        if dpos < ROPE_DIM:
            tab[0, l] = inv[dpos % half]
    return tab


def _rope_kernel(pos_ref, invf_ref, cos_ref, sin_ref):
    half = ROPE_DIM // 2
    ang = pos_ref[0] * invf_ref[...]
    dpos = lax.broadcasted_iota(jnp.int32, ang.shape, 1) & (B_HEAD_DIM - 1)
    sinv = jnp.sin(ang)
    cos_ref[0] = jnp.cos(ang)
    sin_ref[0] = jnp.where(dpos < half, -sinv, jnp.where(dpos < ROPE_DIM, sinv, 0.0))


def _rope_tables(positions, tm):
    bsz, t = positions.shape
    pos_b = jnp.broadcast_to(positions.astype(F32)[:, :, None], (bsz, t, LANES))
    spec = _row_spec(tm, LANES)
    return pl.pallas_call(
        _rope_kernel,
        grid=(bsz, t // tm),
        in_specs=[spec, pl.BlockSpec((1, LANES), lambda b, i: (0, 0))],
        out_specs=[spec, spec],
        out_shape=[jax.ShapeDtypeStruct((bsz, t, LANES), F32)] * 2,
        compiler_params=_cparams(2),
        name="rope_tables",
    )(pos_b, jnp.asarray(_rope_lane_table()))


def _hgrn_prepare(za_ref, lbraw_ref, k_ref, bl_ref, *, layer_e, tq):
    lbraw = lbraw_ref[...]
    lmax = jnp.max(lbraw, axis=0, keepdims=True)
    lexp = jnp.exp(lbraw - lmax)
    lbp = lexp / jnp.sum(lexp, axis=0, keepdims=True)
    lb_all = jnp.zeros((1, A_KD), F32)
    for i in range(1, layer_e + 1):
        lb_all = lb_all + lbp[i:i + 1]

    hc = A_CHUNK // 2
    mid = hc // 2 - 1
    rh_t = lax.broadcasted_iota(jnp.int32, (tq, A_DK), 0) & (hc - 1)
    worst = jnp.zeros((1, A_DK), F32)
    for hd in range(A_HEADS):
        lo = hd * A_DK
        lb = lb_all[:, lo:lo + A_DK]
        sp, sn = _sigmoid_pair(za_ref[:, A_KD + lo:A_KD + lo + A_DK])
        k_ref[hd] = (1.0 - lb) * sn
        bl = jnp.log(lb + (1.0 - lb) * sp)
        s = 1
        while s < hc:
            bl = bl + jnp.where(rh_t >= s, pltpu.roll(bl, s, 0), 0.0)
            s *= 2
        bl_ref[hd] = bl
        for r0 in range(0, tq, hc):
            b_mid = bl[r0 + mid:r0 + mid + 1]
            worst = jnp.maximum(worst, jnp.maximum(-b_mid, b_mid - bl[r0 + hc - 1:r0 + hc]))
    return jnp.max(worst) <= HGRN_FACTORED_MAX_DECAY


def _hgrn_chunks(factored, za_ref, ng_ref, st_ref, k_ref, bl_ref, out_ref, *, tq):
    c = A_CHUNK
    hc = c // 2
    mid = hc // 2 - 1
    rows = lax.broadcasted_iota(jnp.int32, (c, A_DK), 0)
    upper_half = rows >= hc
    sub = rows & (SUBLANES - 1)
    ri = lax.broadcasted_iota(jnp.int32, (c, c), 0)
    ci = lax.broadcasted_iota(jnp.int32, (c, c), 1)
    ng = ng_ref[...]

    def inputs(ch, hd):
        r0 = ch * c
        qc = za_ref[r0:r0 + c, hd * A_DK:(hd + 1) * A_DK]
        vc = za_ref[r0:r0 + c, 2 * A_KD + hd * A_DV:2 * A_KD + (hd + 1) * A_DV]
        return qc, vc, k_ref[hd, r0:r0 + c], bl_ref[hd, r0:r0 + c]

    def emit(ch, hd, o):
        r0 = ch * c
        ag = za_ref[r0:r0 + c, 2 * A_KD + A_WIDTH + hd * A_DV:2 * A_KD + A_WIDTH + (hd + 1) * A_DV]
        o = o * lax.rsqrt(jnp.mean(o * o, axis=-1, keepdims=True) + RMS_EPS) * ng
        out_ref[r0:r0 + c, hd * A_DV:(hd + 1) * A_DV] = (o * _silu(ag)).astype(out_ref.dtype)

    units = [(ch, hd) for ch in range(tq // c) for hd in range(A_HEADS)]

    if factored:
        for ch, hd in units:
            qc, vc, kc, bl = inputs(ch, hd)
            st = st_ref[hd]
            m0, t0 = bl[mid:mid + 1], bl[hc - 1:hc]
            m1, t1 = bl[hc + mid:hc + mid + 1], bl[c - 1:c]
            bm = bl - jnp.where(upper_half, m1, m0)
            qd = qc * jnp.exp(bm)
            kd = kc * jnp.exp(-bm)
            qdb = qd.astype(BF16)
            vb = vc.astype(BF16)
            a_in = _dot_nt(qdb, kd.astype(BF16))
            a_x = _dot_nt(qdb, (kd * jnp.exp(m1 + t0 - m0)).astype(BF16))
            att = jnp.where(((ri ^ ci) < hc) & (ci <= ri), a_in,
                            jnp.where((ri >= hc) & (ci < hc), a_x, 0.0))
            qe = qd * jnp.where(upper_half, jnp.exp(m1 + t0), jnp.exp(m0))
            o = _dot_nt(qe.astype(BF16), st.astype(BF16)) + _dot(att.astype(BF16), vb)
            ke = (kd * jnp.where(upper_half, jnp.exp(t1 - m1), jnp.exp(t0 + t1 - m0))).astype(BF16)
            st_ref[hd] = jnp.exp(t0 + t1) * st + _dot_tn(vb, ke)
            emit(ch, hd, o)
        return

    for ch, hd in units:
        qc, vc, kc, bl = inputs(ch, hd)
        st = st_ref[hd]
        vb = vc.astype(BF16)
        bc = bl + jnp.where(upper_half, bl[hc - 1:hc], 0.0)
        b_last = bc[c - 1:c]
        o = _dot_nt((qc * jnp.exp(bc)).astype(BF16), st.astype(BF16))
        att = jnp.zeros((c, c), F32)
        m = hc
        while m >= SUBLANES:
            bref = jnp.concatenate(
                [jnp.broadcast_to(bc[2 * m * j + m - 1:2 * m * j + m], (2 * m, A_DK))
                 for j in range(c // (2 * m))], axis=0)
            upper = (rows & m) != 0
            qm = jnp.where(upper, qc * jnp.exp(jnp.minimum(bc - bref, 0.0)), 0.0)
            km = jnp.where(upper, 0.0, kc * jnp.exp(jnp.minimum(bref - bc, 0.0)))
            a = _dot_nt(qm.astype(BF16), km.astype(BF16))
            att = att + jnp.where((ri ^ ci) < 2 * m, a, 0.0)
            m //= 2
        o = o + _dot(att.astype(BF16), vb)
        for dd in range(SUBLANES):
            if dd == 0:
                pd = qc * kc
                vs = vc
            else:
                ks = pltpu.roll(kc, dd, 0)
                bs = pltpu.roll(bc, dd, 0)
                vs = pltpu.roll(vc, dd, 0)
                pd = qc * ks * jnp.exp(jnp.minimum(bc - bs, 0.0))
                pd = jnp.where(sub >= dd, pd, 0.0)
            o = o + jnp.sum(pd, axis=-1, keepdims=True) * vs
        ke = (kc * jnp.exp(b_last - bc)).astype(BF16)
        st_ref[hd] = jnp.exp(b_last) * st + _dot_tn(vb, ke)
        emit(ch, hd, o)


def _swa_tile(zb_ref, cos_ref, sin_ref, sink_ref, kp_ref, vp_ref, out_ref, *, layer_e, ti, tq, col0):
    w = WINDOW
    half = ROPE_DIM // 2
    hd = B_HEAD_DIM
    ngrp = B_WIDTH // LANES
    assert 2 * hd == LANES and B_KV_WIDTH == LANES and (B_Q_HEADS // B_KV_HEADS) == 4

    cosv = cos_ref[0]
    sinv = sin_ref[0]
    lane = lax.broadcasted_iota(jnp.int32, (tq, LANES), 1)
    first_half = (lane & (hd - 1)) < half

    def rope(xx):
        partner = jnp.where(first_half, pltpu.roll(xx, LANES - half, 1), pltpu.roll(xx, half, 1))
        return xx * cosv + partner * sinv

    k_new = rope(zb_ref[:, B_WIDTH:B_WIDTH + B_KV_WIDTH])
    v_new = zb_ref[:, B_WIDTH + B_KV_WIDTH:B_WIDTH + 2 * B_KV_WIDTH]
    kext = jnp.concatenate([kp_ref[...], k_new], axis=0)
    vext = jnp.concatenate([vp_ref[...], v_new], axis=0)
    kp_ref[...] = k_new[tq - w:]
    vp_ref[...] = v_new[tq - w:]

    lo_lanes = lax.broadcasted_iota(jnp.int32, kext.shape, 1) < hd

    def lane_variants(a):
        sw = pltpu.roll(a, hd, 1)
        return [[jnp.where(lo_lanes, a, 0.0).astype(BF16), jnp.where(lo_lanes, 0.0, sw).astype(BF16)],
                [jnp.where(lo_lanes, sw, 0.0).astype(BF16), jnp.where(lo_lanes, 0.0, a).astype(BF16)]]

    kvar = lane_variants(kext)
    vvar = lane_variants(vext)

    qs = [(rope(zb_ref[:, j * LANES:(j + 1) * LANES]) * (hd ** -0.5)).astype(BF16)
          for j in range(ngrp)]

    qi = (lax.broadcasted_iota(jnp.int32, (2 * w, 2 * w), 0) & (w - 1)) + w
    ki = lax.broadcasted_iota(jnp.int32, (2 * w, 2 * w), 1)
    rel = qi - ki
    allowed = (rel >= 0) & (rel < w)
    bias = jnp.where(allowed, 0.0, NEG_BIG)
    bias_first = jnp.where(allowed & ((ti > 0) | (ki >= w)), 0.0, NEG_BIG)
    top = lax.broadcasted_iota(jnp.int32, (2 * w, 1), 0) < w
    lo_out = lax.broadcasted_iota(jnp.int32, (2 * w, LANES), 1) < hd
    lo_rows = lax.broadcasted_iota(jnp.int32, (4 * w, LANES), 0) < 2 * w
    lo_cols = lax.broadcasted_iota(jnp.int32, (4 * w, LANES), 1) < hd
    ones_sel = jnp.where(lo_rows == lo_cols, 1.0, 0.0).astype(BF16)

    units = [(j, hk) for j in range(tq // w) for hk in range(B_KV_HEADS)]
    stage1 = []
    for j, hk in units:
        bj = bias_first if j == 0 else bias
        qa = jnp.concatenate([qs[2 * hk][j * w:(j + 1) * w], qs[2 * hk + 1][j * w:(j + 1) * w]],
                             axis=0)
        for par in range(2):
            sink = jnp.where(top, sink_ref[layer_e, 4 * hk + par], sink_ref[layer_e, 4 * hk + 2 + par])
            s = _dot_nt(qa, kvar[hk][par][j * w:(j + 2) * w]) + bj
            mx = jnp.maximum(jnp.max(s, axis=-1, keepdims=True), sink)
            stage1.append((s, mx, sink))
    yield
    stage2 = []
    for s, mx, sink in stage1:
        stage2.append((jnp.exp(s - mx).astype(BF16), jnp.exp(sink - mx)))
    yield
    for ui, (j, hk) in enumerate(units):
        (p0, e0), (p1, e1) = stage2[2 * ui], stage2[2 * ui + 1]
        pp = jnp.concatenate([p0, p1], axis=1)
        vv = jnp.concatenate([vvar[hk][0][j * w:(j + 2) * w], vvar[hk][1][j * w:(j + 2) * w]],
                             axis=0)
        od = _dot(pp, jnp.concatenate([vv, ones_sel], axis=1))
        den = od[:, LANES:] + jnp.where(lo_out, e0, e1)
        o = (od[:, :LANES] / den).astype(out_ref.dtype)
        c0 = col0 + 2 * hk * LANES
        out_ref[j * w:(j + 1) * w, c0:c0 + LANES] = o[:w]
        out_ref[j * w:(j + 1) * w, c0 + LANES:c0 + 2 * LANES] = o[w:]


def _even_kernel(x_ref, mod_ref, win_ref, wout_ref, lbraw_ref, ng_ref, cos_ref, sin_ref, sink_ref,
                 lng_ref, lnb_ref, o_ref,
                 za_ref, zb_ref, oc_ref, st_ref, k_ref, bl_ref, kp_ref, vp_ref,
                 *, layer_e, tq, alpha):
    ti = pl.program_id(1)
    na = 2 * A_KD + 2 * A_WIDTH

    @pl.when(ti == 0)
    def _():
        st_ref[...] = jnp.zeros_like(st_ref)
        kp_ref[...] = jnp.zeros_like(kp_ref)
        vp_ref[...] = jnp.zeros_like(vp_ref)

    mod = mod_ref[0]
    h = (x_ref[0] * (1.0 + mod[1:2]) + mod[0:1]).astype(BF16)

    def project(lo, width, dst_ref, dst_lo):
        dst_ref[:, dst_lo:dst_lo + width] = _dot(h, win_ref[:, lo:lo + width])

    project(na, B_WIDTH + 2 * B_KV_WIDTH, zb_ref, 0)
    project(A_KD, A_KD, za_ref, A_KD)
    swa = _swa_tile(zb_ref, cos_ref, sin_ref, sink_ref, kp_ref, vp_ref, oc_ref,
                    layer_e=layer_e, ti=ti, tq=tq, col0=A_WIDTH)
    next(swa)
    project(0, A_KD, za_ref, 0)
    project(2 * A_KD, A_WIDTH, za_ref, 2 * A_KD)
    factored_ok = _hgrn_prepare(za_ref, lbraw_ref, k_ref, bl_ref, layer_e=layer_e, tq=tq)
    next(swa)
    project(2 * A_KD + A_WIDTH, A_WIDTH, za_ref, 2 * A_KD + A_WIDTH)
    for _ in swa:
        pass

    def finish(factored):
        y = _dot(oc_ref[:, A_WIDTH:], wout_ref[A_WIDTH:, :])
        _hgrn_chunks(factored, za_ref, ng_ref, st_ref, k_ref, bl_ref, oc_ref, tq=tq)
        y = y + _dot(oc_ref[:, :A_WIDTH], wout_ref[:A_WIDTH, :])
        o_ref[0] = _layer_norm(alpha * x_ref[0] + mod[2:3] * y, lng_ref[...], lnb_ref[...])

    pl.when(factored_ok)(functools.partial(finish, True))
    pl.when(jnp.logical_not(factored_ok))(functools.partial(finish, False))


def _even_layer(x, mod_all, w_in, w_out, lb_raw, norm_g, cos_t, sin_t, sinks, ln_g, ln_b,
                *, layer, alpha, tq):
    bsz, t, d = x.shape
    e = layer // 2
    n_in = w_in.shape[-1]
    w = WINDOW
    assert tq % w == 0 and tq % A_CHUNK == 0
    const2 = lambda b, i: (0, 0)
    return pl.pallas_call(
        functools.partial(_even_kernel, layer_e=e, tq=tq, alpha=alpha),
        grid=(bsz, t // tq),
        in_specs=[
            _row_spec(tq, d),
            _mod_spec(layer, d),
            _layer_spec((d, n_in), e),
            _layer_spec((A_WIDTH + B_WIDTH, d), e),
            pl.BlockSpec(lb_raw.shape, const2),
            _layer_spec((1, A_DV), e),
            _row_spec(tq, LANES),
            _row_spec(tq, LANES),
            pl.BlockSpec(memory_space=pltpu.SMEM),
            _layer_spec((1, d), 2 * layer),
            _layer_spec((1, d), 2 * layer),
        ],
        out_specs=_row_spec(tq, d),
        out_shape=jax.ShapeDtypeStruct((bsz, t, d), F32),
        scratch_shapes=[
            pltpu.VMEM((tq, 2 * A_KD + 2 * A_WIDTH), F32),
            pltpu.VMEM((tq, B_WIDTH + 2 * B_KV_WIDTH), F32),
            pltpu.VMEM((tq, A_WIDTH + B_WIDTH), BF16),
            pltpu.VMEM((A_HEADS, A_DV, A_DK), F32),
            pltpu.VMEM((A_HEADS, tq, A_DK), F32),
            pltpu.VMEM((A_HEADS, tq, A_DK), F32),
            pltpu.VMEM((w, B_KV_WIDTH), F32),
            pltpu.VMEM((w, B_KV_WIDTH), F32),
        ],
        compiler_params=_cparams(2),
        name="even_layer",
    )(x, mod_all, w_in, w_out, lb_raw, norm_g, cos_t, sin_t, sinks, ln_g, ln_b)


def _pool_rows(x_ref, mod, w_ref, scale_ref, lng_ref, lnb_ref, carry_ref, y_ref, out_ref,
               *, t0, alpha):
    n, d = x_ref.shape
    gw = d // len(POOL_WINDOWS)
    t_glob = t0 + lax.broadcasted_iota(jnp.int32, (n, 1), 0)
    for gi, win in enumerate(POOL_WINDOWS):
        cols = slice(gi * gw, (gi + 1) * gw)
        h = x_ref[:, cols] * (1.0 + mod[1:2, cols]) + mod[0:1, cols]
        e = jnp.concatenate([carry_ref[:, cols], h], axis=0)
        carry_ref[:, cols] = h[n - POOL_HALO:]
        s = 1
        while s < win:
            e = e + pltpu.roll(e, s, 0)
            s *= 2
        inv_cnt = 1.0 / jnp.minimum(t_glob + 1, win).astype(F32)
        pooled = e[POOL_HALO:] * inv_cnt - h
        y_ref[:, cols] = _dot(pooled.astype(BF16), w_ref[gi])
        yield
    y = y_ref[...] * scale_ref[...]
    out_ref[...] = _layer_norm(alpha * x_ref[...] + mod[2:3] * y, lng_ref[...], lnb_ref[...])


def _ffn_kernel(*refs, pool, alpha, tm, sub, fc, dff, group):
    if pool:
        (x_ref, mod_ref, pw_ref, psc_ref, plng_ref, plnb_ref, wup_ref, cw_ref, cb_ref, wdn_ref,
         lng_ref, lnb_ref, o_ref, tail_ref, ubuf_ref, acc_ref, carry_ref, y_ref, x1_ref) = refs
    else:
        (x_ref, mod_ref, wup_ref, cw_ref, cb_ref, wdn_ref, lng_ref, lnb_ref, o_ref,
         tail_ref, ubuf_ref, acc_ref) = refs
    ti = pl.program_id(1)
    nslab = fc // LANES
    nchunk = dff // fc
    nsub = tm // sub

    @pl.when(ti == 0)
    def _():
        tail_ref[...] = jnp.zeros_like(tail_ref)
        if pool:
            carry_ref[...] = jnp.zeros_like(carry_ref)

    mod = mod_ref[0]
    scale = 1.0 + mod[4:5]
    hs = {}

    def mix(si):
        rows = pl.ds(si * sub, sub)
        return _pool_rows(x_ref.at[0, rows], mod, pw_ref, psc_ref, plng_ref, plnb_ref, carry_ref,
                          y_ref, x1_ref.at[rows], t0=ti * tm + si * sub, alpha=alpha)

    def mlp_in(si):
        rows = slice(si * sub, (si + 1) * sub)
        return x1_ref[rows, :] if pool else x_ref[0, rows, :]

    def up(si, ci):
        if si not in hs:
            hs[si] = (mlp_in(si) * scale + mod[3:4]).astype(BF16)
        lo = ci * fc
        return (_dot(hs[si], wup_ref[:, lo:lo + fc]),
                _dot(hs[si], wup_ref[:, dff + lo:dff + lo + fc]))

    mixing = iter(())
    if pool:
        for _ in mix(0):
            pass
    seq = [(si, ci) for si in range(nsub) for ci in range(nchunk)]
    uv = up(*seq[0])
    parts = []
    for idx, (si, ci) in enumerate(seq):
        u, v = uv
        if pool and ci == 0 and si + 1 < nsub:
            mixing = mix(si + 1)
        if idx + 1 < len(seq):
            if seq[idx + 1][0] != si:
                for _ in mixing:
                    pass
            uv = up(*seq[idx + 1])
        next(mixing, None)
        for s in range(nslab):
            slab = ci * nslab + s
            cl = ci * fc + s * LANES
            us = u[:, s * LANES:(s + 1) * LANES]
            buf = ubuf_ref.at[idx % 2, s]
            buf[0:SUBLANES, :] = tail_ref[slab]
            buf[SUBLANES:SUBLANES + sub, :] = us
            u1 = buf[SUBLANES - 1:SUBLANES - 1 + sub, :]
            u2 = buf[SUBLANES - 2:SUBLANES - 2 + sub, :]
            tail_ref[slab] = us[sub - SUBLANES:]
            cw = cw_ref[:, cl:cl + LANES]
            uc = cb_ref[:, cl:cl + LANES] + u2 * cw[0:1] + u1 * cw[1:2] + us * cw[2:3]
            parts.append((_silu(uc) * v[:, s * LANES:(s + 1) * LANES]).astype(BF16))
        last = ci == nchunk - 1
        if (ci + 1) % group == 0 or last:
            g0 = (ci // group) * group
            part = _dot(jnp.concatenate(parts, axis=1), wdn_ref[g0 * fc:(ci + 1) * fc, :])
            parts = []
            rows = slice(si * sub, (si + 1) * sub)
            if g0 == 0:
                acc_ref[rows, :] = part
            else:
                acc_ref[rows, :] += part
            if last:
                o_ref[0, rows, :] = _layer_norm(
                    alpha * mlp_in(si) + mod[5:6] * acc_ref[rows, :], lng_ref[...], lnb_ref[...])


def _conv_ffn(x, mod_all, w_up, conv_w, conv_b, w_down, ln_g, ln_b, *, layer, alpha, tm, sub,
              pool=None, fc=256, group=None):
    bsz, t, d = x.shape
    dff = w_down.shape[1]
    group = group or dff // fc
    assert dff % fc == 0 and fc % LANES == 0 and CONV_WIDTH == 3 and tm % sub == 0 and sub >= SUBLANES
    in_specs = [_row_spec(tm, d), _mod_spec(layer, d)]
    args = [x, mod_all]
    scratch = [pltpu.VMEM((dff // LANES, SUBLANES, LANES), F32),
               pltpu.VMEM((2, fc // LANES, sub + SUBLANES, LANES), F32),
               pltpu.VMEM((tm, d), F32)]
    if pool is not None:
        w_grp, scale = pool
        _, ng, gw, _ = w_grp.shape
        assert POOL_HALO >= max(POOL_WINDOWS) and sub >= POOL_HALO and ng * gw == d
        assert all(w & (w - 1) == 0 and w >= 2 for w in POOL_WINDOWS)
        in_specs += [_layer_spec((ng, gw, gw), layer // 2), _layer_spec((1, d), layer // 2),
                     _layer_spec((1, d), 2 * layer), _layer_spec((1, d), 2 * layer)]
        args += [w_grp, scale, ln_g, ln_b]
        scratch += [pltpu.VMEM((POOL_HALO, d), F32), pltpu.VMEM((sub, d), F32),
                    pltpu.VMEM((tm, d), F32)]
    in_specs += [
        _layer_spec((d, 2 * dff), layer),
        _layer_spec((CONV_WIDTH, dff), layer),
        _layer_spec((1, dff), layer),
        _layer_spec((dff, d), layer),
        _layer_spec((1, d), 2 * layer + 1),
        _layer_spec((1, d), 2 * layer + 1),
    ]
    args += [w_up, conv_w, conv_b, w_down, ln_g, ln_b]
    return pl.pallas_call(
        functools.partial(_ffn_kernel, pool=pool is not None, alpha=alpha, tm=tm, sub=sub, fc=fc,
                          dff=dff, group=group),
        grid=(bsz, t // tm),
        in_specs=in_specs,
        out_specs=_row_spec(tm, d),
        out_shape=jax.ShapeDtypeStruct((bsz, t, d), F32),
        scratch_shapes=scratch,
        compiler_params=_cparams(2),
        name="ffn_ln" if pool is None else "pool_ffn_ln",
    )(*args)


def _pick_tile(t, pref):
    tm = min(t, pref)
    assert t % tm == 0
    return tm


def kernel(x, c, positions, ada_w, ada_b, ln_g, ln_b, ab_w_in, ab_w_out, hgrn_lb_raw, hgrn_norm_g,
           attn_sinks, pool_w, pool_scale, ffn_w_up, ffn_conv_w, ffn_conv_b, ffn_w_down):
    bsz, t, d = x.shape
    depth = ada_w.shape[0]
    alpha = (2.0 * depth) ** 0.25
    tm = _pick_tile(t, ROW_TILE)
    tq = _pick_tile(t, EVEN_TILE)

    mod_all = _ada_mod(c, ada_w, ada_b).reshape(depth, bsz, 6, d)
    cos_t, sin_t = _rope_tables(positions, _pick_tile(t, ROW_TILE))
    w_in = ab_w_in.astype(BF16)
    w_out = ab_w_out.astype(BF16)
    w_pool = pool_w.astype(BF16)
    w_up = ffn_w_up.astype(BF16)
    w_down = ffn_w_down.astype(BF16)
    ln_g2 = ln_g.reshape(2 * depth, 1, d)
    ln_b2 = ln_b.reshape(2 * depth, 1, d)
    norm_g = hgrn_norm_g.reshape(-1, 1, A_DV)
    pool_sc = pool_scale.reshape(-1, 1, d)
    conv_b = ffn_conv_b.reshape(depth, 1, -1)

    for l in range(depth):
        if l % 2 == 0:
            x = _even_layer(x, mod_all, w_in, w_out, hgrn_lb_raw, norm_g, cos_t, sin_t, attn_sinks,
                            ln_g2, ln_b2, layer=l, alpha=alpha, tq=tq)
            pool = None
        else:
            pool = (w_pool, pool_sc)
        x = _conv_ffn(x, mod_all, w_up, ffn_conv_w, conv_b, w_down, ln_g2, ln_b2, layer=l,
                      alpha=alpha, tm=_pick_tile(t, FFN_TILE), sub=_pick_tile(t, FFN_SUB), pool=pool)
    return x
```

```python
import functools

import numpy as np
import jax
import jax.numpy as jnp
from jax import lax
from jax.experimental import pallas as pl
from jax.experimental.pallas import tpu as pltpu

F32 = jnp.float32
BF16 = jnp.bfloat16

A_HEADS = 4
A_DK = 128
A_DV = 128
A_KD = A_HEADS * A_DK
A_WIDTH = A_HEADS * A_DV
A_CHUNK = 64
B_Q_HEADS = 8
B_KV_HEADS = 2
B_HEAD_DIM = 64
B_WIDTH = B_Q_HEADS * B_HEAD_DIM
B_KV_WIDTH = B_KV_HEADS * B_HEAD_DIM
WINDOW = 128
ROPE_DIM = B_HEAD_DIM // 4
ROPE_THETA = 500000.0
POOL_WINDOWS = (2, 4, 8, 16)
POOL_HALO = 16
CONV_WIDTH = 3
LN_EPS = 1e-5
RMS_EPS = 1e-6

LANES = 128
SUBLANES = 8
VMEM_LIMIT_BYTES = 56 * 1024 * 1024

EVEN_TILE = 512
FFN_TILE = 1024
FFN_SUB = 256

NEG_BIG = -1e30
HGRN_FACTORED_MAX_DECAY = 85.0


def _cparams(n_axes):
    return pltpu.CompilerParams(
        dimension_semantics=("arbitrary",) * n_axes,
        vmem_limit_bytes=VMEM_LIMIT_BYTES,
    )


def _dot(a, b):
    return jnp.dot(a, b, preferred_element_type=F32)


def _dot_nt(a, b):
    return lax.dot_general(a, b, (((1,), (1,)), ((), ())), preferred_element_type=F32)


def _dot_tn(a, b):
    return lax.dot_general(a, b, (((0,), (0,)), ((), ())), preferred_element_type=F32)


def _layer_norm(r, g, b):
    mu = jnp.mean(r, axis=-1, keepdims=True)
    d = r - mu
    var = jnp.mean(d * d, axis=-1, keepdims=True)
    return d * lax.rsqrt(var + LN_EPS) * g + b


def _sigmoid_pair(z):
    e = jnp.exp(-jnp.abs(z))
    r = 1.0 / (1.0 + e)
    er = e * r
    pos = z >= 0
    return jnp.where(pos, r, er), jnp.where(pos, er, r)


def _silu(z):
    return z * (0.5 * jnp.tanh(0.5 * z) + 0.5)


def _row_spec(tm, width):
    return pl.BlockSpec((1, tm, width), lambda b, i: (b, i, 0))


def _layer_spec(shape, layer):
    zeros = (0,) * len(shape)
    return pl.BlockSpec((None,) + tuple(shape), lambda b, i: (layer,) + zeros)


def _mod_spec(layer, d):
    return pl.BlockSpec((None, 1, 6, d), lambda b, i: (layer, b, 0, 0))


def _ada_kernel(c_ref, w_ref, b_ref, o_ref):
    ca = _silu(c_ref[...]).astype(BF16)
    o_ref[0] = _dot(ca, w_ref[0].astype(BF16)) + b_ref[0]


def _ada_mod(c, ada_w, ada_b, tn=1536):
    depth, d, n = ada_w.shape
    bsz = c.shape[0]
    assert n % tn == 0
    return pl.pallas_call(
        _ada_kernel,
        grid=(depth, n // tn),
        in_specs=[
            pl.BlockSpec((bsz, d), lambda l, j: (0, 0)),
            pl.BlockSpec((1, d, tn), lambda l, j: (l, 0, j)),
            pl.BlockSpec((1, 1, tn), lambda l, j: (l, 0, j)),
        ],
        out_specs=pl.BlockSpec((1, bsz, tn), lambda l, j: (l, 0, j)),
        out_shape=jax.ShapeDtypeStruct((depth, bsz, n), F32),
        compiler_params=_cparams(2),
        name="ada_mod",
    )(c, ada_w, ada_b.reshape(depth, 1, n))


ROPE_FREQS = ROPE_DIM // 2
ROPE_ROWS = 2 * SUBLANES


def _rope_expanders():
    ec = np.zeros((ROPE_ROWS, LANES), np.float32)
    es = np.zeros((ROPE_ROWS, LANES), np.float32)
    for l in range(LANES):
        dpos = l % B_HEAD_DIM
        if dpos < ROPE_DIM:
            ec[dpos % ROPE_FREQS, l] = 1.0
            es[dpos % ROPE_FREQS, l] = -1.0 if dpos < ROPE_FREQS else 1.0
        else:
            ec[ROPE_FREQS, l] = 1.0
    return ec, es


def _rope_kernel(pos_ref, invf_ref, ec_ref, es_ref, cos_ref, sin_ref):
    ang = invf_ref[...] * pos_ref[0]
    row = lax.broadcasted_iota(jnp.int32, ang.shape, 0)
    cos_t = jnp.where(row < ROPE_FREQS, jnp.cos(ang), jnp.where(row == ROPE_FREQS, 1.0, 0.0))
    sin_t = jnp.where(row < ROPE_FREQS, jnp.sin(ang), 0.0)

    def spread(a, e_ref):
        hi = a.astype(BF16)
        rest = a - hi.astype(F32)
        mid = rest.astype(BF16)
        lo = (rest - mid.astype(F32)).astype(BF16)
        e = e_ref[...].astype(BF16)
        return _dot_tn(hi, e) + _dot_tn(mid, e) + _dot_tn(lo, e)

    cos_ref[0] = spread(cos_t, ec_ref)
    sin_ref[0] = spread(sin_t, es_ref)


def _rope_tables(positions):
    bsz, t = positions.shape
    inv = np.zeros((ROPE_ROWS, 1), np.float32)
    inv[:ROPE_FREQS, 0] = ROPE_THETA ** (-np.arange(0, ROPE_DIM, 2, dtype=np.float64) / ROPE_DIM)
    ec, es = _rope_expanders()
    out = pl.BlockSpec((1, t, LANES), lambda b: (b, 0, 0))
    const = lambda b: (0, 0)
    return pl.pallas_call(
        _rope_kernel,
        grid=(bsz,),
        in_specs=[pl.BlockSpec((1, 1, t), lambda b: (b, 0, 0)),
                  pl.BlockSpec((ROPE_ROWS, 1), const),
                  pl.BlockSpec((ROPE_ROWS, LANES), const),
                  pl.BlockSpec((ROPE_ROWS, LANES), const)],
        out_specs=[out, out],
        out_shape=[jax.ShapeDtypeStruct((bsz, t, LANES), F32)] * 2,
        compiler_params=_cparams(1),
        name="rope_tables",
    )(positions.astype(F32).reshape(bsz, 1, t), jnp.asarray(inv), jnp.asarray(ec), jnp.asarray(es))


def _hgrn_prepare(za_ref, lbraw_ref, k_ref, bl_ref, *, layer_e, tq):
    lbraw = lbraw_ref[...]
    lmax = jnp.max(lbraw, axis=0, keepdims=True)
    lexp = jnp.exp(lbraw - lmax)
    lbp = lexp / jnp.sum(lexp, axis=0, keepdims=True)
    lb_all = jnp.zeros((1, A_KD), F32)
    for i in range(1, layer_e + 1):
        lb_all = lb_all + lbp[i:i + 1]

    hc = A_CHUNK // 2
    mid = hc // 2 - 1
    rh_t = lax.broadcasted_iota(jnp.int32, (tq, A_DK), 0) & (hc - 1)
    worst = jnp.zeros((1, A_DK), F32)
    for hd in range(A_HEADS):
        lo = hd * A_DK
        lb = lb_all[:, lo:lo + A_DK]
        sp, sn = _sigmoid_pair(za_ref[:, A_KD + lo:A_KD + lo + A_DK])
        k_ref[hd] = (1.0 - lb) * sn
        bl = jnp.log(lb + (1.0 - lb) * sp)
        s = 1
        while s < hc:
            bl = bl + jnp.where(rh_t >= s, pltpu.roll(bl, s, 0), 0.0)
            s *= 2
        bl_ref[hd] = bl
        for r0 in range(0, tq, hc):
            b_mid = bl[r0 + mid:r0 + mid + 1]
            worst = jnp.maximum(worst, jnp.maximum(-b_mid, b_mid - bl[r0 + hc - 1:r0 + hc]))
    return jnp.max(worst) <= HGRN_FACTORED_MAX_DECAY


def _hgrn_chunks(factored, za_ref, ng_ref, st_ref, k_ref, bl_ref, out_ref, *, tq):
    c = A_CHUNK
    hc = c // 2
    mid = hc // 2 - 1
    rows = lax.broadcasted_iota(jnp.int32, (c, A_DK), 0)
    upper_half = rows >= hc
    sub = rows & (SUBLANES - 1)
    ri = lax.broadcasted_iota(jnp.int32, (c, c), 0)
    ci = lax.broadcasted_iota(jnp.int32, (c, c), 1)
    ng = ng_ref[...]

    def inputs(ch, hd):
        r0 = ch * c
        qc = za_ref[r0:r0 + c, hd * A_DK:(hd + 1) * A_DK]
        vc = za_ref[r0:r0 + c, 2 * A_KD + hd * A_DV:2 * A_KD + (hd + 1) * A_DV]
        return qc, vc, k_ref[hd, r0:r0 + c], bl_ref[hd, r0:r0 + c]

    def emit(ch, hd, o):
        r0 = ch * c
        ag = za_ref[r0:r0 + c, 2 * A_KD + A_WIDTH + hd * A_DV:2 * A_KD + A_WIDTH + (hd + 1) * A_DV]
        o = o * lax.rsqrt(jnp.mean(o * o, axis=-1, keepdims=True) + RMS_EPS) * ng
        out_ref[r0:r0 + c, hd * A_DV:(hd + 1) * A_DV] = (o * _silu(ag)).astype(out_ref.dtype)

    units = [(ch, hd) for ch in range(tq // c) for hd in range(A_HEADS)]

    if factored:
        for ch, hd in units:
            qc, vc, kc, bl = inputs(ch, hd)
            st = st_ref[hd]
            m0, t0 = bl[mid:mid + 1], bl[hc - 1:hc]
            m1, t1 = bl[hc + mid:hc + mid + 1], bl[c - 1:c]
            bm = bl - jnp.where(upper_half, m1, m0)
            qd = qc * jnp.exp(bm)
            kd = kc * jnp.exp(-bm)
            qdb = qd.astype(BF16)
            vb = vc.astype(BF16)
            a_in = _dot_nt(qdb, kd.astype(BF16))
            a_x = _dot_nt(qdb, (kd * jnp.exp(m1 + t0 - m0)).astype(BF16))
            att = jnp.where(((ri ^ ci) < hc) & (ci <= ri), a_in,
                            jnp.where((ri >= hc) & (ci < hc), a_x, 0.0))
            qe = qd * jnp.where(upper_half, jnp.exp(m1 + t0), jnp.exp(m0))
            o = _dot_nt(qe.astype(BF16), st.astype(BF16)) + _dot(att.astype(BF16), vb)
            ke = (kd * jnp.where(upper_half, jnp.exp(t1 - m1), jnp.exp(t0 + t1 - m0))).astype(BF16)
            st_ref[hd] = jnp.exp(t0 + t1) * st + _dot_tn(vb, ke)
            emit(ch, hd, o)
        return

    for ch, hd in units:
        qc, vc, kc, bl = inputs(ch, hd)
        st = st_ref[hd]
        vb = vc.astype(BF16)
        bc = bl + jnp.where(upper_half, bl[hc - 1:hc], 0.0)
        b_last = bc[c - 1:c]
        o = _dot_nt((qc * jnp.exp(bc)).astype(BF16), st.astype(BF16))
        att = jnp.zeros((c, c), F32)
        m = hc
        while m >= SUBLANES:
            bref = jnp.concatenate(
                [jnp.broadcast_to(bc[2 * m * j + m - 1:2 * m * j + m], (2 * m, A_DK))
                 for j in range(c // (2 * m))], axis=0)
            upper = (rows & m) != 0
            qm = jnp.where(upper, qc * jnp.exp(jnp.minimum(bc - bref, 0.0)), 0.0)
            km = jnp.where(upper, 0.0, kc * jnp.exp(jnp.minimum(bref - bc, 0.0)))
            a = _dot_nt(qm.astype(BF16), km.astype(BF16))
            att = att + jnp.where((ri ^ ci) < 2 * m, a, 0.0)
            m //= 2
        o = o + _dot(att.astype(BF16), vb)
        for dd in range(SUBLANES):
            if dd == 0:
                pd = qc * kc
                vs = vc
            else:
                ks = pltpu.roll(kc, dd, 0)
                bs = pltpu.roll(bc, dd, 0)
                vs = pltpu.roll(vc, dd, 0)
                pd = qc * ks * jnp.exp(jnp.minimum(bc - bs, 0.0))
                pd = jnp.where(sub >= dd, pd, 0.0)
            o = o + jnp.sum(pd, axis=-1, keepdims=True) * vs
        ke = (kc * jnp.exp(b_last - bc)).astype(BF16)
        st_ref[hd] = jnp.exp(b_last) * st + _dot_tn(vb, ke)
        emit(ch, hd, o)


def _swa_tile(zb_ref, cos_ref, sin_ref, sink_ref, kp_ref, vp_ref, out_ref, *, layer_e, ti, tq, col0):
    w = WINDOW
    half = ROPE_DIM // 2
    hd = B_HEAD_DIM
    ngrp = B_WIDTH // LANES
    assert 2 * hd == LANES and B_KV_WIDTH == LANES and (B_Q_HEADS // B_KV_HEADS) == 4

    cosv = cos_ref[0]
    sinv = sin_ref[0]
    lane = lax.broadcasted_iota(jnp.int32, (tq, LANES), 1)
    first_half = (lane & (hd - 1)) < half

    def rope(xx):
        partner = jnp.where(first_half, pltpu.roll(xx, LANES - half, 1), pltpu.roll(xx, half, 1))
        return xx * cosv + partner * sinv

    k_new = rope(zb_ref[:, B_WIDTH:B_WIDTH + B_KV_WIDTH])
    v_new = zb_ref[:, B_WIDTH + B_KV_WIDTH:B_WIDTH + 2 * B_KV_WIDTH]
    kext = jnp.concatenate([kp_ref[...], k_new], axis=0)
    vext = jnp.concatenate([vp_ref[...], v_new], axis=0)
    kp_ref[...] = k_new[tq - w:]
    vp_ref[...] = v_new[tq - w:]

    lo_lanes = lax.broadcasted_iota(jnp.int32, kext.shape, 1) < hd

    def lane_variants(a):
        sw = pltpu.roll(a, hd, 1)
        return [[jnp.where(lo_lanes, a, 0.0).astype(BF16), jnp.where(lo_lanes, 0.0, sw).astype(BF16)],
                [jnp.where(lo_lanes, sw, 0.0).astype(BF16), jnp.where(lo_lanes, 0.0, a).astype(BF16)]]

    kvar = lane_variants(kext)
    vvar = lane_variants(vext)

    qs = [(rope(zb_ref[:, j * LANES:(j + 1) * LANES]) * (hd ** -0.5)).astype(BF16)
          for j in range(ngrp)]

    qi = (lax.broadcasted_iota(jnp.int32, (2 * w, 2 * w), 0) & (w - 1)) + w
    ki = lax.broadcasted_iota(jnp.int32, (2 * w, 2 * w), 1)
    rel = qi - ki
    allowed = (rel >= 0) & (rel < w)
    bias = jnp.where(allowed, 0.0, NEG_BIG)
    bias_first = jnp.where(allowed & ((ti > 0) | (ki >= w)), 0.0, NEG_BIG)
    top = lax.broadcasted_iota(jnp.int32, (2 * w, 1), 0) < w
    lo_out = lax.broadcasted_iota(jnp.int32, (2 * w, LANES), 1) < hd
    lo_rows = lax.broadcasted_iota(jnp.int32, (4 * w, LANES), 0) < 2 * w
    lo_cols = lax.broadcasted_iota(jnp.int32, (4 * w, LANES), 1) < hd
    ones_sel = jnp.where(lo_rows == lo_cols, 1.0, 0.0).astype(BF16)

    units = [(j, hk) for j in range(tq // w) for hk in range(B_KV_HEADS)]
    stage1 = []
    for j, hk in units:
        bj = bias_first if j == 0 else bias
        qa = jnp.concatenate([qs[2 * hk][j * w:(j + 1) * w], qs[2 * hk + 1][j * w:(j + 1) * w]],
                             axis=0)
        for par in range(2):
            sink = jnp.where(top, sink_ref[layer_e, 4 * hk + par], sink_ref[layer_e, 4 * hk + 2 + par])
            s = _dot_nt(qa, kvar[hk][par][j * w:(j + 2) * w]) + bj
            mx = jnp.maximum(jnp.max(s, axis=-1, keepdims=True), sink)
            stage1.append((s, mx, sink))
    yield
    stage2 = []
    for s, mx, sink in stage1:
        stage2.append((jnp.exp(s - mx).astype(BF16), jnp.exp(sink - mx)))
    yield
    for ui, (j, hk) in enumerate(units):
        (p0, e0), (p1, e1) = stage2[2 * ui], stage2[2 * ui + 1]
        pp = jnp.concatenate([p0, p1], axis=1)
        vv = jnp.concatenate([vvar[hk][0][j * w:(j + 2) * w], vvar[hk][1][j * w:(j + 2) * w]],
                             axis=0)
        od = _dot(pp, jnp.concatenate([vv, ones_sel], axis=1))
        den = od[:, LANES:] + jnp.where(lo_out, e0, e1)
        o = (od[:, :LANES] / den).astype(out_ref.dtype)
        c0 = col0 + 2 * hk * LANES
        out_ref[j * w:(j + 1) * w, c0:c0 + LANES] = o[:w]
        out_ref[j * w:(j + 1) * w, c0 + LANES:c0 + 2 * LANES] = o[w:]


def _even_kernel(x_ref, mod_ref, win_ref, wout_ref, lbraw_ref, ng_ref, cos_ref, sin_ref, sink_ref,
                 lng_ref, lnb_ref, o_ref,
                 za_ref, zb_ref, oc_ref, st_ref, k_ref, bl_ref, kp_ref, vp_ref,
                 *, layer_e, tq, alpha):
    ti = pl.program_id(1)
    na = 2 * A_KD + 2 * A_WIDTH

    @pl.when(ti == 0)
    def _():
        st_ref[...] = jnp.zeros_like(st_ref)
        kp_ref[...] = jnp.zeros_like(kp_ref)
        vp_ref[...] = jnp.zeros_like(vp_ref)

    mod = mod_ref[0]
    h = (x_ref[0] * (1.0 + mod[1:2]) + mod[0:1]).astype(BF16)

    def project(lo, width, dst_ref, dst_lo):
        dst_ref[:, dst_lo:dst_lo + width] = _dot(h, win_ref[:, lo:lo + width])

    project(na, B_WIDTH + 2 * B_KV_WIDTH, zb_ref, 0)
    project(A_KD, A_KD, za_ref, A_KD)
    swa = _swa_tile(zb_ref, cos_ref, sin_ref, sink_ref, kp_ref, vp_ref, oc_ref,
                    layer_e=layer_e, ti=ti, tq=tq, col0=A_WIDTH)
    next(swa)
    project(0, A_KD, za_ref, 0)
    project(2 * A_KD, A_WIDTH, za_ref, 2 * A_KD)
    factored_ok = _hgrn_prepare(za_ref, lbraw_ref, k_ref, bl_ref, layer_e=layer_e, tq=tq)
    next(swa)
    project(2 * A_KD + A_WIDTH, A_WIDTH, za_ref, 2 * A_KD + A_WIDTH)
    for _ in swa:
        pass

    def finish(factored):
        y = _dot(oc_ref[:, A_WIDTH:], wout_ref[A_WIDTH:, :])
        _hgrn_chunks(factored, za_ref, ng_ref, st_ref, k_ref, bl_ref, oc_ref, tq=tq)
        y = y + _dot(oc_ref[:, :A_WIDTH], wout_ref[:A_WIDTH, :])
        o_ref[0] = _layer_norm(alpha * x_ref[0] + mod[2:3] * y, lng_ref[...], lnb_ref[...])

    pl.when(factored_ok)(functools.partial(finish, True))
    pl.when(jnp.logical_not(factored_ok))(functools.partial(finish, False))


def _even_layer(x, mod_all, w_in, w_out, lb_raw, norm_g, cos_t, sin_t, sinks, ln_g, ln_b,
                *, layer, alpha, tq):
    bsz, t, d = x.shape
    e = layer // 2
    n_in = w_in.shape[-1]
    w = WINDOW
    assert tq % w == 0 and tq % A_CHUNK == 0
    const2 = lambda b, i: (0, 0)
    return pl.pallas_call(
        functools.partial(_even_kernel, layer_e=e, tq=tq, alpha=alpha),
        grid=(bsz, t // tq),
        in_specs=[
            _row_spec(tq, d),
            _mod_spec(layer, d),
            _layer_spec((d, n_in), e),
            _layer_spec((A_WIDTH + B_WIDTH, d), e),
            pl.BlockSpec(lb_raw.shape, const2),
            _layer_spec((1, A_DV), e),
            _row_spec(tq, LANES),
            _row_spec(tq, LANES),
            pl.BlockSpec(memory_space=pltpu.SMEM),
            _layer_spec((1, d), 2 * layer),
            _layer_spec((1, d), 2 * layer),
        ],
        out_specs=_row_spec(tq, d),
        out_shape=jax.ShapeDtypeStruct((bsz, t, d), F32),
        scratch_shapes=[
            pltpu.VMEM((tq, 2 * A_KD + 2 * A_WIDTH), F32),
            pltpu.VMEM((tq, B_WIDTH + 2 * B_KV_WIDTH), F32),
            pltpu.VMEM((tq, A_WIDTH + B_WIDTH), BF16),
            pltpu.VMEM((A_HEADS, A_DV, A_DK), F32),
            pltpu.VMEM((A_HEADS, tq, A_DK), F32),
            pltpu.VMEM((A_HEADS, tq, A_DK), F32),
            pltpu.VMEM((w, B_KV_WIDTH), F32),
            pltpu.VMEM((w, B_KV_WIDTH), F32),
        ],
        compiler_params=_cparams(2),
        name="even_layer",
    )(x, mod_all, w_in, w_out, lb_raw, norm_g, cos_t, sin_t, sinks, ln_g, ln_b)


def _pool_rows(x_ref, mod, w_ref, scale_ref, lng_ref, lnb_ref, carry_ref, y_ref, out_ref,
               *, t0, alpha):
    n, d = x_ref.shape
    gw = d // len(POOL_WINDOWS)
    t_glob = t0 + lax.broadcasted_iota(jnp.int32, (n, 1), 0)
    for gi, win in enumerate(POOL_WINDOWS):
        cols = slice(gi * gw, (gi + 1) * gw)
        h = x_ref[:, cols] * (1.0 + mod[1:2, cols]) + mod[0:1, cols]
        e = jnp.concatenate([carry_ref[:, cols], h], axis=0)
        carry_ref[:, cols] = h[n - POOL_HALO:]
        s = 1
        while s < win:
            e = e + pltpu.roll(e, s, 0)
            s *= 2
        inv_cnt = 1.0 / jnp.minimum(t_glob + 1, win).astype(F32)
        pooled = e[POOL_HALO:] * inv_cnt - h
        y_ref[:, cols] = _dot(pooled.astype(BF16), w_ref[gi])
        yield
    y = y_ref[...] * scale_ref[...]
    out_ref[...] = _layer_norm(alpha * x_ref[...] + mod[2:3] * y, lng_ref[...], lnb_ref[...])


def _ffn_kernel(*refs, pool, alpha, tm, sub, fc, dff, group):
    if pool:
        (x_ref, mod_ref, pw_ref, psc_ref, plng_ref, plnb_ref, wup_ref, cw_ref, cb_ref, wdn_ref,
         lng_ref, lnb_ref, o_ref, tail_ref, ubuf_ref, acc_ref, carry_ref, y_ref, x1_ref) = refs
    else:
        (x_ref, mod_ref, wup_ref, cw_ref, cb_ref, wdn_ref, lng_ref, lnb_ref, o_ref,
         tail_ref, ubuf_ref, acc_ref) = refs
    ti = pl.program_id(1)
    nslab = fc // LANES
    nchunk = dff // fc
    nsub = tm // sub

    @pl.when(ti == 0)
    def _():
        tail_ref[...] = jnp.zeros_like(tail_ref)
        if pool:
            carry_ref[...] = jnp.zeros_like(carry_ref)

    mod = mod_ref[0]
    scale = 1.0 + mod[4:5]
    hs = {}

    def mix(si):
        rows = pl.ds(si * sub, sub)
        return _pool_rows(x_ref.at[0, rows], mod, pw_ref, psc_ref, plng_ref, plnb_ref, carry_ref,
                          y_ref, x1_ref.at[rows], t0=ti * tm + si * sub, alpha=alpha)

    def mlp_in(si):
        rows = slice(si * sub, (si + 1) * sub)
        return x1_ref[rows, :] if pool else x_ref[0, rows, :]

    def up(si, ci):
        if si not in hs:
            hs[si] = (mlp_in(si) * scale + mod[3:4]).astype(BF16)
        lo = ci * fc
        return (_dot(hs[si], wup_ref[:, lo:lo + fc]),
                _dot(hs[si], wup_ref[:, dff + lo:dff + lo + fc]))

    mixing = iter(())
    if pool:
        for _ in mix(0):
            pass
    seq = [(si, ci) for si in range(nsub) for ci in range(nchunk)]
    uv = up(*seq[0])
    parts = []
    for idx, (si, ci) in enumerate(seq):
        u, v = uv
        if pool and ci == 0 and si + 1 < nsub:
            mixing = mix(si + 1)
        if idx + 1 < len(seq):
            if seq[idx + 1][0] != si:
                for _ in mixing:
                    pass
            uv = up(*seq[idx + 1])
        next(mixing, None)
        for s in range(nslab):
            slab = ci * nslab + s
            cl = ci * fc + s * LANES
            us = u[:, s * LANES:(s + 1) * LANES]
            buf = ubuf_ref.at[idx % 2, s]
            buf[0:SUBLANES, :] = tail_ref[slab]
            buf[SUBLANES:SUBLANES + sub, :] = us
            u1 = buf[SUBLANES - 1:SUBLANES - 1 + sub, :]
            u2 = buf[SUBLANES - 2:SUBLANES - 2 + sub, :]
            tail_ref[slab] = us[sub - SUBLANES:]
            cw = cw_ref[:, cl:cl + LANES]
            uc = cb_ref[:, cl:cl + LANES] + u2 * cw[0:1] + u1 * cw[1:2] + us * cw[2:3]
            parts.append((_silu(uc) * v[:, s * LANES:(s + 1) * LANES]).astype(BF16))
        last = ci == nchunk - 1
        if (ci + 1) % group == 0 or last:
            g0 = (ci // group) * group
            part = _dot(jnp.concatenate(parts, axis=1), wdn_ref[g0 * fc:(ci + 1) * fc, :])
            parts = []
            rows = slice(si * sub, (si + 1) * sub)
            if g0 == 0:
                acc_ref[rows, :] = part
            else:
                acc_ref[rows, :] += part
            if last:
                o_ref[0, rows, :] = _layer_norm(
                    alpha * mlp_in(si) + mod[5:6] * acc_ref[rows, :], lng_ref[...], lnb_ref[...])


def _conv_ffn(x, mod_all, w_up, conv_w, conv_b, w_down, ln_g, ln_b, *, layer, alpha, tm, sub,
              pool=None, fc=256, group=None):
    bsz, t, d = x.shape
    dff = w_down.shape[1]
    group = group or dff // fc
    assert dff % fc == 0 and fc % LANES == 0 and CONV_WIDTH == 3 and tm % sub == 0 and sub >= SUBLANES
    in_specs = [_row_spec(tm, d), _mod_spec(layer, d)]
    args = [x, mod_all]
    scratch = [pltpu.VMEM((dff // LANES, SUBLANES, LANES), F32),
               pltpu.VMEM((2, fc // LANES, sub + SUBLANES, LANES), F32),
               pltpu.VMEM((tm, d), F32)]
    if pool is not None:
        w_grp, scale = pool
        _, ng, gw, _ = w_grp.shape
        assert POOL_HALO >= max(POOL_WINDOWS) and sub >= POOL_HALO and ng * gw == d
        assert all(w & (w - 1) == 0 and w >= 2 for w in POOL_WINDOWS)
        in_specs += [_layer_spec((ng, gw, gw), layer // 2), _layer_spec((1, d), layer // 2),
                     _layer_spec((1, d), 2 * layer), _layer_spec((1, d), 2 * layer)]
        args += [w_grp, scale, ln_g, ln_b]
        scratch += [pltpu.VMEM((POOL_HALO, d), F32), pltpu.VMEM((sub, d), F32),
                    pltpu.VMEM((tm, d), F32)]
    in_specs += [
        _layer_spec((d, 2 * dff), layer),
        _layer_spec((CONV_WIDTH, dff), layer),
        _layer_spec((1, dff), layer),
        _layer_spec((dff, d), layer),
        _layer_spec((1, d), 2 * layer + 1),
        _layer_spec((1, d), 2 * layer + 1),
    ]
    args += [w_up, conv_w, conv_b, w_down, ln_g, ln_b]
    return pl.pallas_call(
        functools.partial(_ffn_kernel, pool=pool is not None, alpha=alpha, tm=tm, sub=sub, fc=fc,
                          dff=dff, group=group),
        grid=(bsz, t // tm),
        in_specs=in_specs,
        out_specs=_row_spec(tm, d),
        out_shape=jax.ShapeDtypeStruct((bsz, t, d), F32),
        scratch_shapes=scratch,
        compiler_params=_cparams(2),
        name="ffn_ln" if pool is None else "pool_ffn_ln",
    )(*args)


def _pick_tile(t, pref):
    tm = min(t, pref)
    assert t % tm == 0
    return tm


def kernel(x, c, positions, ada_w, ada_b, ln_g, ln_b, ab_w_in, ab_w_out, hgrn_lb_raw, hgrn_norm_g,
           attn_sinks, pool_w, pool_scale, ffn_w_up, ffn_conv_w, ffn_conv_b, ffn_w_down):
    bsz, t, d = x.shape
    depth = ada_w.shape[0]
    alpha = (2.0 * depth) ** 0.25
    tq = _pick_tile(t, EVEN_TILE)

    mod_all = _ada_mod(c, ada_w, ada_b).reshape(depth, bsz, 6, d)
    cos_t, sin_t = _rope_tables(positions)
    w_in = ab_w_in.astype(BF16)
    w_out = ab_w_out.astype(BF16)
    w_pool = pool_w.astype(BF16)
    w_up = ffn_w_up.astype(BF16)
    w_down = ffn_w_down.astype(BF16)
    ln_g2 = ln_g.reshape(2 * depth, 1, d)
    ln_b2 = ln_b.reshape(2 * depth, 1, d)
    norm_g = hgrn_norm_g.reshape(-1, 1, A_DV)
    pool_sc = pool_scale.reshape(-1, 1, d)
    conv_b = ffn_conv_b.reshape(depth, 1, -1)

    for l in range(depth):
        if l % 2 == 0:
            x = _even_layer(x, mod_all, w_in, w_out, hgrn_lb_raw, norm_g, cos_t, sin_t, attn_sinks,
                            ln_g2, ln_b2, layer=l, alpha=alpha, tq=tq)
            pool = None
        else:
            pool = (w_pool, pool_sc)
        x = _conv_ffn(x, mod_all, w_up, ffn_conv_w, conv_b, w_down, ln_g2, ln_b2, layer=l,
                      alpha=alpha, tm=_pick_tile(t, FFN_TILE), sub=_pick_tile(t, FFN_SUB), pool=pool)
    return x
```

```python
import functools

import numpy as np
import jax
import jax.numpy as jnp
from jax import lax
from jax.experimental import pallas as pl
from jax.experimental.pallas import tpu as pltpu

F32 = jnp.float32
BF16 = jnp.bfloat16

A_HEADS = 4
A_DK = 128
A_DV = 128
A_KD = A_HEADS * A_DK
A_WIDTH = A_HEADS * A_DV
A_CHUNK = 64
B_Q_HEADS = 8
B_KV_HEADS = 2
B_HEAD_DIM = 64
B_WIDTH = B_Q_HEADS * B_HEAD_DIM
B_KV_WIDTH = B_KV_HEADS * B_HEAD_DIM
WINDOW = 128
ROPE_DIM = B_HEAD_DIM // 4
ROPE_THETA = 500000.0
POOL_WINDOWS = (2, 4, 8, 16)
POOL_HALO = 16
CONV_WIDTH = 3
LN_EPS = 1e-5
RMS_EPS = 1e-6

LANES = 128
SUBLANES = 8
VMEM_LIMIT_BYTES = 56 * 1024 * 1024

EVEN_TILE = 512
FFN_TILE = 1024
FFN_SUB = 256

NEG_BIG = -1e30
HGRN_FACTORED_MAX_DECAY = 85.0


def _cparams(n_axes):
    return pltpu.CompilerParams(
        dimension_semantics=("arbitrary",) * n_axes,
        vmem_limit_bytes=VMEM_LIMIT_BYTES,
    )


def _dot(a, b):
    return jnp.dot(a, b, preferred_element_type=F32)


def _dot_nt(a, b):
    return lax.dot_general(a, b, (((1,), (1,)), ((), ())), preferred_element_type=F32)


def _dot_tn(a, b):
    return lax.dot_general(a, b, (((0,), (0,)), ((), ())), preferred_element_type=F32)


def _layer_norm(r, g, b):
    mu = jnp.mean(r, axis=-1, keepdims=True)
    d = r - mu
    var = jnp.mean(d * d, axis=-1, keepdims=True)
    return d * lax.rsqrt(var + LN_EPS) * g + b


def _sigmoid_pair(z):
    e = jnp.exp(-jnp.abs(z))
    r = 1.0 / (1.0 + e)
    er = e * r
    pos = z >= 0
    return jnp.where(pos, r, er), jnp.where(pos, er, r)


def _silu(z):
    return z * (0.5 * jnp.tanh(0.5 * z) + 0.5)


def _row_spec(tm, width):
    return pl.BlockSpec((1, tm, width), lambda b, i: (b, i, 0))


def _layer_spec(shape, layer):
    zeros = (0,) * len(shape)
    return pl.BlockSpec((None,) + tuple(shape), lambda b, i: (layer,) + zeros)


def _mod_spec(layer, d):
    return pl.BlockSpec((None, 1, 6, d), lambda b, i: (layer, b, 0, 0))


def _ada_kernel(c_ref, w_ref, b_ref, o_ref):
    ca = _silu(c_ref[...]).astype(BF16)
    o_ref[0] = _dot(ca, w_ref[0].astype(BF16)) + b_ref[0]


def _ada_mod(c, ada_w, ada_b, tn=1536):
    depth, d, n = ada_w.shape
    bsz = c.shape[0]
    assert n % tn == 0
    return pl.pallas_call(
        _ada_kernel,
        grid=(depth, n // tn),
        in_specs=[
            pl.BlockSpec((bsz, d), lambda l, j: (0, 0)),
            pl.BlockSpec((1, d, tn), lambda l, j: (l, 0, j)),
            pl.BlockSpec((1, 1, tn), lambda l, j: (l, 0, j)),
        ],
        out_specs=pl.BlockSpec((1, bsz, tn), lambda l, j: (l, 0, j)),
        out_shape=jax.ShapeDtypeStruct((depth, bsz, n), F32),
        compiler_params=_cparams(2),
        name="ada_mod",
    )(c, ada_w, ada_b.reshape(depth, 1, n))


ROPE_FREQS = ROPE_DIM // 2
ROPE_ROWS = 2 * SUBLANES


def _rope_expanders():
    ec = np.zeros((ROPE_ROWS, LANES), np.float32)
    es = np.zeros((ROPE_ROWS, LANES), np.float32)
    for l in range(LANES):
        dpos = l % B_HEAD_DIM
        if dpos < ROPE_DIM:
            ec[dpos % ROPE_FREQS, l] = 1.0
            es[dpos % ROPE_FREQS, l] = -1.0 if dpos < ROPE_FREQS else 1.0
        else:
            ec[ROPE_FREQS, l] = 1.0
    return ec, es


def _rope_kernel(pos_ref, invf_ref, ec_ref, es_ref, cos_ref, sin_ref):
    ang = invf_ref[...] * pos_ref[0]
    row = lax.broadcasted_iota(jnp.int32, ang.shape, 0)
    cos_t = jnp.where(row < ROPE_FREQS, jnp.cos(ang), jnp.where(row == ROPE_FREQS, 1.0, 0.0))
    sin_t = jnp.where(row < ROPE_FREQS, jnp.sin(ang), 0.0)

    def spread(a, e_ref):
        hi = a.astype(BF16)
        rest = a - hi.astype(F32)
        mid = rest.astype(BF16)
        lo = (rest - mid.astype(F32)).astype(BF16)
        e = e_ref[...].astype(BF16)
        return _dot_tn(hi, e) + _dot_tn(mid, e) + _dot_tn(lo, e)

    cos_ref[0] = spread(cos_t, ec_ref)
    sin_ref[0] = spread(sin_t, es_ref)


def _rope_tables(positions):
    bsz, t = positions.shape
    inv = np.zeros((ROPE_ROWS, 1), np.float32)
    inv[:ROPE_FREQS, 0] = ROPE_THETA ** (-np.arange(0, ROPE_DIM, 2, dtype=np.float64) / ROPE_DIM)
    ec, es = _rope_expanders()
    out = pl.BlockSpec((1, t, LANES), lambda b: (b, 0, 0))
    const = lambda b: (0, 0)
    return pl.pallas_call(
        _rope_kernel,
        grid=(bsz,),
        in_specs=[pl.BlockSpec((1, 1, t), lambda b: (b, 0, 0)),
                  pl.BlockSpec((ROPE_ROWS, 1), const),
                  pl.BlockSpec((ROPE_ROWS, LANES), const),
                  pl.BlockSpec((ROPE_ROWS, LANES), const)],
        out_specs=[out, out],
        out_shape=[jax.ShapeDtypeStruct((bsz, t, LANES), F32)] * 2,
        compiler_params=_cparams(1),
        name="rope_tables",
    )(positions.astype(F32).reshape(bsz, 1, t), jnp.asarray(inv), jnp.asarray(ec), jnp.asarray(es))


def _hgrn_prepare(za_ref, lbraw_ref, k_ref, bl_ref, *, layer_e, tq):
    lbraw = lbraw_ref[...]
    lmax = jnp.max(lbraw, axis=0, keepdims=True)
    lexp = jnp.exp(lbraw - lmax)
    lbp = lexp / jnp.sum(lexp, axis=0, keepdims=True)
    lb_all = jnp.zeros((1, A_KD), F32)
    for i in range(1, layer_e + 1):
        lb_all = lb_all + lbp[i:i + 1]

    hc = A_CHUNK // 2
    mid = hc // 2 - 1
    rh_t = lax.broadcasted_iota(jnp.int32, (tq, A_DK), 0) & (hc - 1)
    worst = jnp.zeros((1, A_DK), F32)
    for hd in range(A_HEADS):
        lo = hd * A_DK
        lb = lb_all[:, lo:lo + A_DK]
        sp, sn = _sigmoid_pair(za_ref[:, A_KD + lo:A_KD + lo + A_DK])
        k_ref[hd] = (1.0 - lb) * sn
        bl = jnp.log(lb + (1.0 - lb) * sp)
        s = 1
        while s < hc:
            bl = bl + jnp.where(rh_t >= s, pltpu.roll(bl, s, 0), 0.0)
            s *= 2
        bl_ref[hd] = bl
        for r0 in range(0, tq, hc):
            b_mid = bl[r0 + mid:r0 + mid + 1]
            worst = jnp.maximum(worst, jnp.maximum(-b_mid, b_mid - bl[r0 + hc - 1:r0 + hc]))
    return jnp.max(worst) <= HGRN_FACTORED_MAX_DECAY


def _hgrn_chunks(factored, za_ref, ng_ref, st_ref, k_ref, bl_ref, out_ref, *, tq):
    c = A_CHUNK
    hc = c // 2
    mid = hc // 2 - 1
    rows = lax.broadcasted_iota(jnp.int32, (c, A_DK), 0)
    upper_half = rows >= hc
    sub = rows & (SUBLANES - 1)
    ri = lax.broadcasted_iota(jnp.int32, (c, c), 0)
    ci = lax.broadcasted_iota(jnp.int32, (c, c), 1)
    ng = ng_ref[...]

    def inputs(ch, hd):
        r0 = ch * c
        qc = za_ref[r0:r0 + c, hd * A_DK:(hd + 1) * A_DK]
        vc = za_ref[r0:r0 + c, 2 * A_KD + hd * A_DV:2 * A_KD + (hd + 1) * A_DV]
        return qc, vc, k_ref[hd, r0:r0 + c], bl_ref[hd, r0:r0 + c]

    def emit(ch, hd, o):
        r0 = ch * c
        ag = za_ref[r0:r0 + c, 2 * A_KD + A_WIDTH + hd * A_DV:2 * A_KD + A_WIDTH + (hd + 1) * A_DV]
        o = o * lax.rsqrt(jnp.mean(o * o, axis=-1, keepdims=True) + RMS_EPS) * ng
        out_ref[r0:r0 + c, hd * A_DV:(hd + 1) * A_DV] = (o * _silu(ag)).astype(out_ref.dtype)

    units = [(ch, hd) for ch in range(tq // c) for hd in range(A_HEADS)]

    if factored:
        for ch, hd in units:
            qc, vc, kc, bl = inputs(ch, hd)
            st = st_ref[hd]
            m0, t0 = bl[mid:mid + 1], bl[hc - 1:hc]
            m1, t1 = bl[hc + mid:hc + mid + 1], bl[c - 1:c]
            bm = bl - jnp.where(upper_half, m1, m0)
            qd = qc * jnp.exp(bm)
            kd = kc * jnp.exp(-bm)
            qdb = qd.astype(BF16)
            vb = vc.astype(BF16)
            a_in = _dot_nt(qdb, kd.astype(BF16))
            a_x = _dot_nt(qdb, (kd * jnp.exp(m1 + t0 - m0)).astype(BF16))
            att = jnp.where(((ri ^ ci) < hc) & (ci <= ri), a_in,
                            jnp.where((ri >= hc) & (ci < hc), a_x, 0.0))
            qe = qd * jnp.where(upper_half, jnp.exp(m1 + t0), jnp.exp(m0))
            o = _dot_nt(qe.astype(BF16), st.astype(BF16)) + _dot(att.astype(BF16), vb)
            ke = (kd * jnp.where(upper_half, jnp.exp(t1 - m1), jnp.exp(t0 + t1 - m0))).astype(BF16)
            st_ref[hd] = jnp.exp(t0 + t1) * st + _dot_tn(vb, ke)
            emit(ch, hd, o)
        return

    for ch, hd in units:
        qc, vc, kc, bl = inputs(ch, hd)
        st = st_ref[hd]
        vb = vc.astype(BF16)
        bc = bl + jnp.where(upper_half, bl[hc - 1:hc], 0.0)
        b_last = bc[c - 1:c]
        o = _dot_nt((qc * jnp.exp(bc)).astype(BF16), st.astype(BF16))
        att = jnp.zeros((c, c), F32)
        m = hc
        while m >= SUBLANES:
            bref = jnp.concatenate(
                [jnp.broadcast_to(bc[2 * m * j + m - 1:2 * m * j + m], (2 * m, A_DK))
                 for j in range(c // (2 * m))], axis=0)
            upper = (rows & m) != 0
            qm = jnp.where(upper, qc * jnp.exp(jnp.minimum(bc - bref, 0.0)), 0.0)
            km = jnp.where(upper, 0.0, kc * jnp.exp(jnp.minimum(bref - bc, 0.0)))
            a = _dot_nt(qm.astype(BF16), km.astype(BF16))
            att = att + jnp.where((ri ^ ci) < 2 * m, a, 0.0)
            m //= 2
        o = o + _dot(att.astype(BF16), vb)
        for dd in range(SUBLANES):
            if dd == 0:
                pd = qc * kc
                vs = vc
            else:
                ks = pltpu.roll(kc, dd, 0)
                bs = pltpu.roll(bc, dd, 0)
                vs = pltpu.roll(vc, dd, 0)
                pd = qc * ks * jnp.exp(jnp.minimum(bc - bs, 0.0))
                pd = jnp.where(sub >= dd, pd, 0.0)
            o = o + jnp.sum(pd, axis=-1, keepdims=True) * vs
        ke = (kc * jnp.exp(b_last - bc)).astype(BF16)
        st_ref[hd] = jnp.exp(b_last) * st + _dot_tn(vb, ke)
        emit(ch, hd, o)


def _swa_tile(zb_ref, cos_ref, sin_ref, sink_ref, kp_ref, vp_ref, out_ref, *, layer_e, ti, tq, col0):
    w = WINDOW
    half = ROPE_DIM // 2
    hd = B_HEAD_DIM
    ngrp = B_WIDTH // LANES
    assert 2 * hd == LANES and B_KV_WIDTH == LANES and (B_Q_HEADS // B_KV_HEADS) == 4

    cosv = cos_ref[0]
    sinv = sin_ref[0]
    lane = lax.broadcasted_iota(jnp.int32, (tq, LANES), 1)
    first_half = (lane & (hd - 1)) < half

    def rope(xx):
        partner = jnp.where(first_half, pltpu.roll(xx, LANES - half, 1), pltpu.roll(xx, half, 1))
        return xx * cosv + partner * sinv

    k_new = rope(zb_ref[:, B_WIDTH:B_WIDTH + B_KV_WIDTH])
    v_new = zb_ref[:, B_WIDTH + B_KV_WIDTH:B_WIDTH + 2 * B_KV_WIDTH]
    kext = jnp.concatenate([kp_ref[...], k_new], axis=0)
    vext = jnp.concatenate([vp_ref[...], v_new], axis=0)
    kp_ref[...] = k_new[tq - w:]
    vp_ref[...] = v_new[tq - w:]

    lo_lanes = lax.broadcasted_iota(jnp.int32, kext.shape, 1) < hd

    def lane_variants(a):
        sw = pltpu.roll(a, hd, 1)
        return [[jnp.where(lo_lanes, a, 0.0).astype(BF16), jnp.where(lo_lanes, 0.0, sw).astype(BF16)],
                [jnp.where(lo_lanes, sw, 0.0).astype(BF16), jnp.where(lo_lanes, 0.0, a).astype(BF16)]]

    kvar = lane_variants(kext)
    vvar = lane_variants(vext)

    qs = [(rope(zb_ref[:, j * LANES:(j + 1) * LANES]) * (hd ** -0.5)).astype(BF16)
          for j in range(ngrp)]

    qi = (lax.broadcasted_iota(jnp.int32, (2 * w, 2 * w), 0) & (w - 1)) + w
    ki = lax.broadcasted_iota(jnp.int32, (2 * w, 2 * w), 1)
    rel = qi - ki
    allowed = (rel >= 0) & (rel < w)
    bias = jnp.where(allowed, 0.0, NEG_BIG)
    bias_first = jnp.where(allowed & ((ti > 0) | (ki >= w)), 0.0, NEG_BIG)
    top = lax.broadcasted_iota(jnp.int32, (2 * w, 1), 0) < w
    lo_out = lax.broadcasted_iota(jnp.int32, (2 * w, LANES), 1) < hd
    lo_rows = lax.broadcasted_iota(jnp.int32, (4 * w, LANES), 0) < 2 * w
    lo_cols = lax.broadcasted_iota(jnp.int32, (4 * w, LANES), 1) < hd
    ones_sel = jnp.where(lo_rows == lo_cols, 1.0, 0.0).astype(BF16)

    units = [(j, hk) for j in range(tq // w) for hk in range(B_KV_HEADS)]
    stage1 = []
    for j, hk in units:
        bj = bias_first if j == 0 else bias
        qa = jnp.concatenate([qs[2 * hk][j * w:(j + 1) * w], qs[2 * hk + 1][j * w:(j + 1) * w]],
                             axis=0)
        for par in range(2):
            sink = jnp.where(top, sink_ref[layer_e, 4 * hk + par], sink_ref[layer_e, 4 * hk + 2 + par])
            s = _dot_nt(qa, kvar[hk][par][j * w:(j + 2) * w]) + bj
            mx = jnp.maximum(jnp.max(s, axis=-1, keepdims=True), sink)
            stage1.append((s, mx, sink))
    yield
    stage2 = []
    for s, mx, sink in stage1:
        stage2.append((jnp.exp(s - mx).astype(BF16), jnp.exp(sink - mx)))
    yield
    for ui, (j, hk) in enumerate(units):
        (p0, e0), (p1, e1) = stage2[2 * ui], stage2[2 * ui + 1]
        pp = jnp.concatenate([p0, p1], axis=1)
        vv = jnp.concatenate([vvar[hk][0][j * w:(j + 2) * w], vvar[hk][1][j * w:(j + 2) * w]],
                             axis=0)
        od = _dot(pp, jnp.concatenate([vv, ones_sel], axis=1))
        den = od[:, LANES:] + jnp.where(lo_out, e0, e1)
        o = (od[:, :LANES] / den).astype(out_ref.dtype)
        c0 = col0 + 2 * hk * LANES
        out_ref[j * w:(j + 1) * w, c0:c0 + LANES] = o[:w]
        out_ref[j * w:(j + 1) * w, c0 + LANES:c0 + 2 * LANES] = o[w:]


def _even_kernel(x_ref, mod_ref, win_ref, wout_ref, lbraw_ref, ng_ref, cos_ref, sin_ref, sink_ref,
                 lng_ref, lnb_ref, o_ref,
                 za_ref, zb_ref, oc_ref, st_ref, k_ref, bl_ref, kp_ref, vp_ref,
                 *, layer_e, tq, alpha):
    ti = pl.program_id(1)
    na = 2 * A_KD + 2 * A_WIDTH

    @pl.when(ti == 0)
    def _():
        st_ref[...] = jnp.zeros_like(st_ref)
        kp_ref[...] = jnp.zeros_like(kp_ref)
        vp_ref[...] = jnp.zeros_like(vp_ref)

    mod = mod_ref[0]
    h = (x_ref[0] * (1.0 + mod[1:2]) + mod[0:1]).astype(BF16)

    def project(lo, width, dst_ref, dst_lo):
        dst_ref[:, dst_lo:dst_lo + width] = _dot(h, win_ref[:, lo:lo + width])

    project(na, B_WIDTH + 2 * B_KV_WIDTH, zb_ref, 0)
    project(A_KD, A_KD, za_ref, A_KD)
    project(0, A_KD, za_ref, 0)
    project(2 * A_KD, A_WIDTH, za_ref, 2 * A_KD)
    project(2 * A_KD + A_WIDTH, A_WIDTH, za_ref, 2 * A_KD + A_WIDTH)
    swa = _swa_tile(zb_ref, cos_ref, sin_ref, sink_ref, kp_ref, vp_ref, oc_ref,
                    layer_e=layer_e, ti=ti, tq=tq, col0=A_WIDTH)
    next(swa)
    factored_ok = _hgrn_prepare(za_ref, lbraw_ref, k_ref, bl_ref, layer_e=layer_e, tq=tq)
    for _ in swa:
        pass

    def finish(factored):
        y = _dot(oc_ref[:, A_WIDTH:], wout_ref[A_WIDTH:, :])
        _hgrn_chunks(factored, za_ref, ng_ref, st_ref, k_ref, bl_ref, oc_ref, tq=tq)
        y = y + _dot(oc_ref[:, :A_WIDTH], wout_ref[:A_WIDTH, :])
        o_ref[0] = _layer_norm(alpha * x_ref[0] + mod[2:3] * y, lng_ref[...], lnb_ref[...])

    pl.when(factored_ok)(functools.partial(finish, True))
    pl.when(jnp.logical_not(factored_ok))(functools.partial(finish, False))


def _even_layer(x, mod_all, w_in, w_out, lb_raw, norm_g, cos_t, sin_t, sinks, ln_g, ln_b,
                *, layer, alpha, tq):
    bsz, t, d = x.shape
    e = layer // 2
    n_in = w_in.shape[-1]
    w = WINDOW
    assert tq % w == 0 and tq % A_CHUNK == 0
    const2 = lambda b, i: (0, 0)
    return pl.pallas_call(
        functools.partial(_even_kernel, layer_e=e, tq=tq, alpha=alpha),
        grid=(bsz, t // tq),
        in_specs=[
            _row_spec(tq, d),
            _mod_spec(layer, d),
            _layer_spec((d, n_in), e),
            _layer_spec((A_WIDTH + B_WIDTH, d), e),
            pl.BlockSpec(lb_raw.shape, const2),
            _layer_spec((1, A_DV), e),
            _row_spec(tq, LANES),
            _row_spec(tq, LANES),
            pl.BlockSpec(memory_space=pltpu.SMEM),
            _layer_spec((1, d), 2 * layer),
            _layer_spec((1, d), 2 * layer),
        ],
        out_specs=_row_spec(tq, d),
        out_shape=jax.ShapeDtypeStruct((bsz, t, d), F32),
        scratch_shapes=[
            pltpu.VMEM((tq, 2 * A_KD + 2 * A_WIDTH), F32),
            pltpu.VMEM((tq, B_WIDTH + 2 * B_KV_WIDTH), F32),
            pltpu.VMEM((tq, A_WIDTH + B_WIDTH), BF16),
            pltpu.VMEM((A_HEADS, A_DV, A_DK), F32),
            pltpu.VMEM((A_HEADS, tq, A_DK), F32),
            pltpu.VMEM((A_HEADS, tq, A_DK), F32),
            pltpu.VMEM((w, B_KV_WIDTH), F32),
            pltpu.VMEM((w, B_KV_WIDTH), F32),
        ],
        compiler_params=_cparams(2),
        name="even_layer",
    )(x, mod_all, w_in, w_out, lb_raw, norm_g, cos_t, sin_t, sinks, ln_g, ln_b)


def _pool_rows(x_ref, mod, w_ref, scale_ref, lng_ref, lnb_ref, carry_ref, y_ref, out_ref,
               *, t0, alpha):
    n, d = x_ref.shape
    gw = d // len(POOL_WINDOWS)
    t_glob = t0 + lax.broadcasted_iota(jnp.int32, (n, 1), 0)
    for gi, win in enumerate(POOL_WINDOWS):
        cols = slice(gi * gw, (gi + 1) * gw)
        h = x_ref[:, cols] * (1.0 + mod[1:2, cols]) + mod[0:1, cols]
        e = jnp.concatenate([carry_ref[:, cols], h], axis=0)
        carry_ref[:, cols] = h[n - POOL_HALO:]
        s = 1
        while s < win:
            e = e + pltpu.roll(e, s, 0)
            s *= 2
        inv_cnt = 1.0 / jnp.minimum(t_glob + 1, win).astype(F32)
        pooled = e[POOL_HALO:] * inv_cnt - h
        y_ref[:, cols] = _dot(pooled.astype(BF16), w_ref[gi])
        yield
    y = y_ref[...] * scale_ref[...]
    out_ref[...] = _layer_norm(alpha * x_ref[...] + mod[2:3] * y, lng_ref[...], lnb_ref[...])


def _ffn_kernel(*refs, pool, alpha, tm, sub, fc, dff, group):
    if pool:
        (x_ref, mod_ref, pw_ref, psc_ref, plng_ref, plnb_ref, wup_ref, cw_ref, cb_ref, wdn_ref,
         lng_ref, lnb_ref, o_ref, tail_ref, ubuf_ref, acc_ref, carry_ref, y_ref, x1_ref) = refs
    else:
        (x_ref, mod_ref, wup_ref, cw_ref, cb_ref, wdn_ref, lng_ref, lnb_ref, o_ref,
         tail_ref, ubuf_ref, acc_ref) = refs
    ti = pl.program_id(1)
    nslab = fc // LANES
    nchunk = dff // fc
    nsub = tm // sub

    @pl.when(ti == 0)
    def _():
        tail_ref[...] = jnp.zeros_like(tail_ref)
        if pool:
            carry_ref[...] = jnp.zeros_like(carry_ref)

    mod = mod_ref[0]
    scale = 1.0 + mod[4:5]
    hs = {}

    def mix(si):
        rows = pl.ds(si * sub, sub)
        return _pool_rows(x_ref.at[0, rows], mod, pw_ref, psc_ref, plng_ref, plnb_ref, carry_ref,
                          y_ref, x1_ref.at[rows], t0=ti * tm + si * sub, alpha=alpha)

    def mlp_in(si):
        rows = slice(si * sub, (si + 1) * sub)
        return x1_ref[rows, :] if pool else x_ref[0, rows, :]

    def up(si, ci):
        if si not in hs:
            hs[si] = (mlp_in(si) * scale + mod[3:4]).astype(BF16)
        lo = ci * fc
        return (_dot(hs[si], wup_ref[:, lo:lo + fc]),
                _dot(hs[si], wup_ref[:, dff + lo:dff + lo + fc]))

    mixing = iter(())
    if pool:
        for _ in mix(0):
            pass
    seq = [(si, ci) for si in range(nsub) for ci in range(nchunk)]
    uv = up(*seq[0])
    parts = []
    for idx, (si, ci) in enumerate(seq):
        u, v = uv
        if pool and ci == 0 and si + 1 < nsub:
            mixing = mix(si + 1)
        if idx + 1 < len(seq):
            if seq[idx + 1][0] != si:
                for _ in mixing:
                    pass
            uv = up(*seq[idx + 1])
        next(mixing, None)
        for s in range(nslab):
            slab = ci * nslab + s
            cl = ci * fc + s * LANES
            us = u[:, s * LANES:(s + 1) * LANES]
            buf = ubuf_ref.at[idx % 2, s]
            buf[0:SUBLANES, :] = tail_ref[slab]
            buf[SUBLANES:SUBLANES + sub, :] = us
            u1 = buf[SUBLANES - 1:SUBLANES - 1 + sub, :]
            u2 = buf[SUBLANES - 2:SUBLANES - 2 + sub, :]
            tail_ref[slab] = us[sub - SUBLANES:]
            cw = cw_ref[:, cl:cl + LANES]
            uc = cb_ref[:, cl:cl + LANES] + u2 * cw[0:1] + u1 * cw[1:2] + us * cw[2:3]
            parts.append((_silu(uc) * v[:, s * LANES:(s + 1) * LANES]).astype(BF16))
        last = ci == nchunk - 1
        if (ci + 1) % group == 0 or last:
            g0 = (ci // group) * group
            part = _dot(jnp.concatenate(parts, axis=1), wdn_ref[g0 * fc:(ci + 1) * fc, :])
            parts = []
            rows = slice(si * sub, (si + 1) * sub)
            if g0 == 0:
                acc_ref[rows, :] = part
            else:
                acc_ref[rows, :] += part
            if last:
                o_ref[0, rows, :] = _layer_norm(
                    alpha * mlp_in(si) + mod[5:6] * acc_ref[rows, :], lng_ref[...], lnb_ref[...])


def _conv_ffn(x, mod_all, w_up, conv_w, conv_b, w_down, ln_g, ln_b, *, layer, alpha, tm, sub,
              pool=None, fc=256, group=None):
    bsz, t, d = x.shape
    dff = w_down.shape[1]
    group = group or dff // fc
    assert dff % fc == 0 and fc % LANES == 0 and CONV_WIDTH == 3 and tm % sub == 0 and sub >= SUBLANES
    in_specs = [_row_spec(tm, d), _mod_spec(layer, d)]
    args = [x, mod_all]
    scratch = [pltpu.VMEM((dff // LANES, SUBLANES, LANES), F32),
               pltpu.VMEM((2, fc // LANES, sub + SUBLANES, LANES), F32),
               pltpu.VMEM((tm, d), F32)]
    if pool is not None:
        w_grp, scale = pool
        _, ng, gw, _ = w_grp.shape
        assert POOL_HALO >= max(POOL_WINDOWS) and sub >= POOL_HALO and ng * gw == d
        assert all(w & (w - 1) == 0 and w >= 2 for w in POOL_WINDOWS)
        in_specs += [_layer_spec((ng, gw, gw), layer // 2), _layer_spec((1, d), layer // 2),
                     _layer_spec((1, d), 2 * layer), _layer_spec((1, d), 2 * layer)]
        args += [w_grp, scale, ln_g, ln_b]
        scratch += [pltpu.VMEM((POOL_HALO, d), F32), pltpu.VMEM((sub, d), F32),
                    pltpu.VMEM((tm, d), F32)]
    in_specs += [
        _layer_spec((d, 2 * dff), layer),
        _layer_spec((CONV_WIDTH, dff), layer),
        _layer_spec((1, dff), layer),
        _layer_spec((dff, d), layer),
        _layer_spec((1, d), 2 * layer + 1),
        _layer_spec((1, d), 2 * layer + 1),
    ]
    args += [w_up, conv_w, conv_b, w_down, ln_g, ln_b]
    return pl.pallas_call(
        functools.partial(_ffn_kernel, pool=pool is not None, alpha=alpha, tm=tm, sub=sub, fc=fc,
                          dff=dff, group=group),
        grid=(bsz, t // tm),
        in_specs=in_specs,
        out_specs=_row_spec(tm, d),
        out_shape=jax.ShapeDtypeStruct((bsz, t, d), F32),
        scratch_shapes=scratch,
        compiler_params=_cparams(2),
        name="ffn_ln" if pool is None else "pool_ffn_ln",
    )(*args)


def _pick_tile(t, pref):
    tm = min(t, pref)
    assert t % tm == 0
    return tm


def kernel(x, c, positions, ada_w, ada_b, ln_g, ln_b, ab_w_in, ab_w_out, hgrn_lb_raw, hgrn_norm_g,
           attn_sinks, pool_w, pool_scale, ffn_w_up, ffn_conv_w, ffn_conv_b, ffn_w_down):
    bsz, t, d = x.shape
    depth = ada_w.shape[0]
    alpha = (2.0 * depth) ** 0.25
    tq = _pick_tile(t, EVEN_TILE)

    mod_all = _ada_mod(c, ada_w, ada_b).reshape(depth, bsz, 6, d)
    cos_t, sin_t = _rope_tables(positions)
    w_in = ab_w_in.astype(BF16)
    w_out = ab_w_out.astype(BF16)
    w_pool = pool_w.astype(BF16)
    w_up = ffn_w_up.astype(BF16)
    w_down = ffn_w_down.astype(BF16)
    ln_g2 = ln_g.reshape(2 * depth, 1, d)
    ln_b2 = ln_b.reshape(2 * depth, 1, d)
    norm_g = hgrn_norm_g.reshape(-1, 1, A_DV)
    pool_sc = pool_scale.reshape(-1, 1, d)
    conv_b = ffn_conv_b.reshape(depth, 1, -1)

    for l in range(depth):
        if l % 2 == 0:
            x = _even_layer(x, mod_all, w_in, w_out, hgrn_lb_raw, norm_g, cos_t, sin_t, attn_sinks,
                            ln_g2, ln_b2, layer=l, alpha=alpha, tq=tq)
            pool = None
        else:
            pool = (w_pool, pool_sc)
        x = _conv_ffn(x, mod_all, w_up, ffn_conv_w, conv_b, w_down, ln_g2, ln_b2, layer=l,
                      alpha=alpha, tm=_pick_tile(t, FFN_TILE), sub=_pick_tile(t, FFN_SUB), pool=pool)
    return x
```

```python
import functools

import numpy as np
import jax
import jax.numpy as jnp
from jax import lax
from jax.experimental import pallas as pl
from jax.experimental.pallas import tpu as pltpu

F32 = jnp.float32
BF16 = jnp.bfloat16

A_HEADS = 4
A_DK = 128
A_DV = 128
A_KD = A_HEADS * A_DK
A_WIDTH = A_HEADS * A_DV
A_CHUNK = 64
B_Q_HEADS = 8
B_KV_HEADS = 2
B_HEAD_DIM = 64
B_WIDTH = B_Q_HEADS * B_HEAD_DIM
B_KV_WIDTH = B_KV_HEADS * B_HEAD_DIM
WINDOW = 128
ROPE_DIM = B_HEAD_DIM // 4
ROPE_THETA = 500000.0
POOL_WINDOWS = (2, 4, 8, 16)
POOL_HALO = 16
CONV_WIDTH = 3
LN_EPS = 1e-5
RMS_EPS = 1e-6

LANES = 128
SUBLANES = 8
VMEM_LIMIT_BYTES = 56 * 1024 * 1024

EVEN_TILE = 512
FFN_TILE = 1024
FFN_SUB = 256

NEG_BIG = -1e30
HGRN_FACTORED_MAX_DECAY = 85.0


def _cparams(n_axes):
    return pltpu.CompilerParams(
        dimension_semantics=("arbitrary",) * n_axes,
        vmem_limit_bytes=VMEM_LIMIT_BYTES,
    )


def _dot(a, b):
    return jnp.dot(a, b, preferred_element_type=F32)


def _dot_nt(a, b):
    return lax.dot_general(a, b, (((1,), (1,)), ((), ())), preferred_element_type=F32)


def _dot_tn(a, b):
    return lax.dot_general(a, b, (((0,), (0,)), ((), ())), preferred_element_type=F32)


def _layer_norm(r, g, b):
    mu = jnp.mean(r, axis=-1, keepdims=True)
    d = r - mu
    var = jnp.mean(d * d, axis=-1, keepdims=True)
    return d * lax.rsqrt(var + LN_EPS) * g + b


def _sigmoid_pair(z):
    e = jnp.exp(-jnp.abs(z))
    r = 1.0 / (1.0 + e)
    er = e * r
    pos = z >= 0
    return jnp.where(pos, r, er), jnp.where(pos, er, r)


def _silu(z):
    return z * (0.5 * jnp.tanh(0.5 * z) + 0.5)


def _row_spec(tm, width):
    return pl.BlockSpec((1, tm, width), lambda b, i: (b, i, 0))


def _layer_spec(shape, layer):
    zeros = (0,) * len(shape)
    return pl.BlockSpec((None,) + tuple(shape), lambda b, i: (layer,) + zeros)


def _mod_spec(layer, d):
    return pl.BlockSpec((None, 1, 6, d), lambda b, i: (layer, b, 0, 0))


def _ada_kernel(c_ref, w_ref, b_ref, o_ref):
    ca = _silu(c_ref[...]).astype(BF16)
    o_ref[0] = _dot(ca, w_ref[0].astype(BF16)) + b_ref[0]


def _ada_mod(c, ada_w, ada_b, tn=1536):
    depth, d, n = ada_w.shape
    bsz = c.shape[0]
    assert n % tn == 0
    return pl.pallas_call(
        _ada_kernel,
        grid=(depth, n // tn),
        in_specs=[
            pl.BlockSpec((bsz, d), lambda l, j: (0, 0)),
            pl.BlockSpec((1, d, tn), lambda l, j: (l, 0, j)),
            pl.BlockSpec((1, 1, tn), lambda l, j: (l, 0, j)),
        ],
        out_specs=pl.BlockSpec((1, bsz, tn), lambda l, j: (l, 0, j)),
        out_shape=jax.ShapeDtypeStruct((depth, bsz, n), F32),
        compiler_params=_cparams(2),
        name="ada_mod",
    )(c, ada_w, ada_b.reshape(depth, 1, n))


ROPE_FREQS = ROPE_DIM // 2
ROPE_ROWS = 2 * SUBLANES


def _rope_expanders():
    ec = np.zeros((ROPE_ROWS, LANES), np.float32)
    es = np.zeros((ROPE_ROWS, LANES), np.float32)
    for l in range(LANES):
        dpos = l % B_HEAD_DIM
        if dpos < ROPE_DIM:
            ec[dpos % ROPE_FREQS, l] = 1.0
            es[dpos % ROPE_FREQS, l] = -1.0 if dpos < ROPE_FREQS else 1.0
        else:
            ec[ROPE_FREQS, l] = 1.0
    return ec, es


def _rope_kernel(pos_ref, invf_ref, ec_ref, es_ref, cos_ref, sin_ref):
    ang = invf_ref[...] * pos_ref[0]
    row = lax.broadcasted_iota(jnp.int32, ang.shape, 0)
    cos_t = jnp.where(row < ROPE_FREQS, jnp.cos(ang), jnp.where(row == ROPE_FREQS, 1.0, 0.0))
    sin_t = jnp.where(row < ROPE_FREQS, jnp.sin(ang), 0.0)

    def spread(a, e_ref):
        hi = a.astype(BF16)
        rest = a - hi.astype(F32)
        mid = rest.astype(BF16)
        lo = (rest - mid.astype(F32)).astype(BF16)
        e = e_ref[...].astype(BF16)
        return _dot_tn(hi, e) + _dot_tn(mid, e) + _dot_tn(lo, e)

    cos_ref[0] = spread(cos_t, ec_ref)
    sin_ref[0] = spread(sin_t, es_ref)


def _rope_tables(positions):
    bsz, t = positions.shape
    inv = np.zeros((ROPE_ROWS, 1), np.float32)
    inv[:ROPE_FREQS, 0] = ROPE_THETA ** (-np.arange(0, ROPE_DIM, 2, dtype=np.float64) / ROPE_DIM)
    ec, es = _rope_expanders()
    out = pl.BlockSpec((1, t, LANES), lambda b: (b, 0, 0))
    const = lambda b: (0, 0)
    return pl.pallas_call(
        _rope_kernel,
        grid=(bsz,),
        in_specs=[pl.BlockSpec((1, 1, t), lambda b: (b, 0, 0)),
                  pl.BlockSpec((ROPE_ROWS, 1), const),
                  pl.BlockSpec((ROPE_ROWS, LANES), const),
                  pl.BlockSpec((ROPE_ROWS, LANES), const)],
        out_specs=[out, out],
        out_shape=[jax.ShapeDtypeStruct((bsz, t, LANES), F32)] * 2,
        compiler_params=_cparams(1),
        name="rope_tables",
    )(positions.astype(F32).reshape(bsz, 1, t), jnp.asarray(inv), jnp.asarray(ec), jnp.asarray(es))


def _hgrn_prepare(za_ref, lbraw_ref, k_ref, bl_ref, *, layer_e, tq):
    lbraw = lbraw_ref[...]
    lmax = jnp.max(lbraw, axis=0, keepdims=True)
    lexp = jnp.exp(lbraw - lmax)
    lbp = lexp / jnp.sum(lexp, axis=0, keepdims=True)
    lb_all = jnp.zeros((1, A_KD), F32)
    for i in range(1, layer_e + 1):
        lb_all = lb_all + lbp[i:i + 1]

    hc = A_CHUNK // 2
    mid = hc // 2 - 1
    rh_t = lax.broadcasted_iota(jnp.int32, (tq, A_DK), 0) & (hc - 1)
    worst = jnp.zeros((1, A_DK), F32)
    for hd in range(A_HEADS):
        lo = hd * A_DK
        lb = lb_all[:, lo:lo + A_DK]
        sp, sn = _sigmoid_pair(za_ref[:, A_KD + lo:A_KD + lo + A_DK])
        k_ref[hd] = (1.0 - lb) * sn
        bl = jnp.log(lb + (1.0 - lb) * sp)
        s = 1
        while s < hc:
            bl = bl + jnp.where(rh_t >= s, pltpu.roll(bl, s, 0), 0.0)
            s *= 2
        bl_ref[hd] = bl
        for r0 in range(0, tq, hc):
            b_mid = bl[r0 + mid:r0 + mid + 1]
            worst = jnp.maximum(worst, jnp.maximum(-b_mid, b_mid - bl[r0 + hc - 1:r0 + hc]))
    return jnp.max(worst) <= HGRN_FACTORED_MAX_DECAY


def _hgrn_chunks(factored, za_ref, ng_ref, st_ref, k_ref, bl_ref, out_ref, *, tq):
    c = A_CHUNK
    hc = c // 2
    mid = hc // 2 - 1
    rows = lax.broadcasted_iota(jnp.int32, (c, A_DK), 0)
    upper_half = rows >= hc
    sub = rows & (SUBLANES - 1)
    ri = lax.broadcasted_iota(jnp.int32, (c, c), 0)
    ci = lax.broadcasted_iota(jnp.int32, (c, c), 1)
    ng = ng_ref[...]

    def inputs(ch, hd):
        r0 = ch * c
        qc = za_ref[r0:r0 + c, hd * A_DK:(hd + 1) * A_DK]
        vc = za_ref[r0:r0 + c, 2 * A_KD + hd * A_DV:2 * A_KD + (hd + 1) * A_DV]
        return qc, vc, k_ref[hd, r0:r0 + c], bl_ref[hd, r0:r0 + c]

    def emit(ch, hd, o):
        r0 = ch * c
        ag = za_ref[r0:r0 + c, 2 * A_KD + A_WIDTH + hd * A_DV:2 * A_KD + A_WIDTH + (hd + 1) * A_DV]
        o = o * lax.rsqrt(jnp.mean(o * o, axis=-1, keepdims=True) + RMS_EPS) * ng
        out_ref[r0:r0 + c, hd * A_DV:(hd + 1) * A_DV] = (o * _silu(ag)).astype(out_ref.dtype)

    units = [(ch, hd) for ch in range(tq // c) for hd in range(A_HEADS)]

    if factored:
        for ch, hd in units:
            qc, vc, kc, bl = inputs(ch, hd)
            st = st_ref[hd]
            m0, t0 = bl[mid:mid + 1], bl[hc - 1:hc]
            m1, t1 = bl[hc + mid:hc + mid + 1], bl[c - 1:c]
            bm = bl - jnp.where(upper_half, m1, m0)
            qd = qc * jnp.exp(bm)
            kd = kc * jnp.exp(-bm)
            qdb = qd.astype(BF16)
            vb = vc.astype(BF16)
            a_in = _dot_nt(qdb, kd.astype(BF16))
            a_x = _dot_nt(qdb, (kd * jnp.exp(m1 + t0 - m0)).astype(BF16))
            att = jnp.where(((ri ^ ci) < hc) & (ci <= ri), a_in,
                            jnp.where((ri >= hc) & (ci < hc), a_x, 0.0))
            qe = qd * jnp.where(upper_half, jnp.exp(m1 + t0), jnp.exp(m0))
            o = _dot_nt(qe.astype(BF16), st.astype(BF16)) + _dot(att.astype(BF16), vb)
            ke = (kd * jnp.where(upper_half, jnp.exp(t1 - m1), jnp.exp(t0 + t1 - m0))).astype(BF16)
            st_ref[hd] = jnp.exp(t0 + t1) * st + _dot_tn(vb, ke)
            emit(ch, hd, o)
        return

    for ch, hd in units:
        qc, vc, kc, bl = inputs(ch, hd)
        st = st_ref[hd]
        vb = vc.astype(BF16)
        bc = bl + jnp.where(upper_half, bl[hc - 1:hc], 0.0)
        b_last = bc[c - 1:c]
        o = _dot_nt((qc * jnp.exp(bc)).astype(BF16), st.astype(BF16))
        att = jnp.zeros((c, c), F32)
        m = hc
        while m >= SUBLANES:
            bref = jnp.concatenate(
                [jnp.broadcast_to(bc[2 * m * j + m - 1:2 * m * j + m], (2 * m, A_DK))
                 for j in range(c // (2 * m))], axis=0)
            upper = (rows & m) != 0
            qm = jnp.where(upper, qc * jnp.exp(jnp.minimum(bc - bref, 0.0)), 0.0)
            km = jnp.where(upper, 0.0, kc * jnp.exp(jnp.minimum(bref - bc, 0.0)))
            a = _dot_nt(qm.astype(BF16), km.astype(BF16))
            att = att + jnp.where((ri ^ ci) < 2 * m, a, 0.0)
            m //= 2
        o = o + _dot(att.astype(BF16), vb)
        for dd in range(SUBLANES):
            if dd == 0:
                pd = qc * kc
                vs = vc
            else:
                ks = pltpu.roll(kc, dd, 0)
                bs = pltpu.roll(bc, dd, 0)
                vs = pltpu.roll(vc, dd, 0)
                pd = qc * ks * jnp.exp(jnp.minimum(bc - bs, 0.0))
                pd = jnp.where(sub >= dd, pd, 0.0)
            o = o + jnp.sum(pd, axis=-1, keepdims=True) * vs
        ke = (kc * jnp.exp(b_last - bc)).astype(BF16)
        st_ref[hd] = jnp.exp(b_last) * st + _dot_tn(vb, ke)
        emit(ch, hd, o)


def _swa_tile(zb_ref, cos_ref, sin_ref, sink_ref, kp_ref, vp_ref, out_ref, *, layer_e, ti, tq, col0):
    w = WINDOW
    half = ROPE_DIM // 2
    hd = B_HEAD_DIM
    ngrp = B_WIDTH // LANES
    assert 2 * hd == LANES and B_KV_WIDTH == LANES and (B_Q_HEADS // B_KV_HEADS) == 4

    cosv = cos_ref[0]
    sinv = sin_ref[0]
    lane = lax.broadcasted_iota(jnp.int32, (tq, LANES), 1)
    first_half = (lane & (hd - 1)) < half

    def rope(xx):
        partner = jnp.where(first_half, pltpu.roll(xx, LANES - half, 1), pltpu.roll(xx, half, 1))
        return xx * cosv + partner * sinv

    k_new = rope(zb_ref[:, B_WIDTH:B_WIDTH + B_KV_WIDTH])
    v_new = zb_ref[:, B_WIDTH + B_KV_WIDTH:B_WIDTH + 2 * B_KV_WIDTH]
    kext = jnp.concatenate([kp_ref[...], k_new], axis=0)
    vext = jnp.concatenate([vp_ref[...], v_new], axis=0)
    kp_ref[...] = k_new[tq - w:]
    vp_ref[...] = v_new[tq - w:]

    lo_lanes = lax.broadcasted_iota(jnp.int32, kext.shape, 1) < hd

    def lane_variants(a):
        sw = pltpu.roll(a, hd, 1)
        return [[jnp.where(lo_lanes, a, 0.0).astype(BF16), jnp.where(lo_lanes, 0.0, sw).astype(BF16)],
                [jnp.where(lo_lanes, sw, 0.0).astype(BF16), jnp.where(lo_lanes, 0.0, a).astype(BF16)]]

    kvar = lane_variants(kext)
    vvar = lane_variants(vext)

    qs = [(rope(zb_ref[:, j * LANES:(j + 1) * LANES]) * (hd ** -0.5)).astype(BF16)
          for j in range(ngrp)]

    qi = (lax.broadcasted_iota(jnp.int32, (2 * w, 2 * w), 0) & (w - 1)) + w
    ki = lax.broadcasted_iota(jnp.int32, (2 * w, 2 * w), 1)
    rel = qi - ki
    allowed = (rel >= 0) & (rel < w)
    bias = jnp.where(allowed, 0.0, NEG_BIG)
    bias_first = jnp.where(allowed & ((ti > 0) | (ki >= w)), 0.0, NEG_BIG)
    top = lax.broadcasted_iota(jnp.int32, (2 * w, 1), 0) < w
    lo_out = lax.broadcasted_iota(jnp.int32, (2 * w, LANES), 1) < hd
    lo_rows = lax.broadcasted_iota(jnp.int32, (4 * w, LANES), 0) < 2 * w
    lo_cols = lax.broadcasted_iota(jnp.int32, (4 * w, LANES), 1) < hd
    ones_sel = jnp.where(lo_rows == lo_cols, 1.0, 0.0).astype(BF16)

    units = [(j, hk) for j in range(tq // w) for hk in range(B_KV_HEADS)]
    stage1 = []
    for j, hk in units:
        bj = bias_first if j == 0 else bias
        qa = jnp.concatenate([qs[2 * hk][j * w:(j + 1) * w], qs[2 * hk + 1][j * w:(j + 1) * w]],
                             axis=0)
        for par in range(2):
            sink = jnp.where(top, sink_ref[layer_e, 4 * hk + par], sink_ref[layer_e, 4 * hk + 2 + par])
            s = _dot_nt(qa, kvar[hk][par][j * w:(j + 2) * w]) + bj
            mx = jnp.maximum(jnp.max(s, axis=-1, keepdims=True), sink)
            stage1.append((s, mx, sink))
    yield
    stage2 = []
    for s, mx, sink in stage1:
        stage2.append((jnp.exp(s - mx).astype(BF16), jnp.exp(sink - mx)))
    yield
    for ui, (j, hk) in enumerate(units):
        (p0, e0), (p1, e1) = stage2[2 * ui], stage2[2 * ui + 1]
        pp = jnp.concatenate([p0, p1], axis=1)
        vv = jnp.concatenate([vvar[hk][0][j * w:(j + 2) * w], vvar[hk][1][j * w:(j + 2) * w]],
                             axis=0)
        od = _dot(pp, jnp.concatenate([vv, ones_sel], axis=1))
        den = od[:, LANES:] + jnp.where(lo_out, e0, e1)
        o = (od[:, :LANES] / den).astype(out_ref.dtype)
        c0 = col0 + 2 * hk * LANES
        out_ref[j * w:(j + 1) * w, c0:c0 + LANES] = o[:w]
        out_ref[j * w:(j + 1) * w, c0 + LANES:c0 + 2 * LANES] = o[w:]


def _even_kernel(x_ref, mod_ref, win_ref, wout_ref, lbraw_ref, ng_ref, cos_ref, sin_ref, sink_ref,
                 lng_ref, lnb_ref, o_ref,
                 za_ref, zb_ref, oc_ref, st_ref, k_ref, bl_ref, kp_ref, vp_ref,
                 *, layer_e, tq, alpha):
    ti = pl.program_id(1)
    na = 2 * A_KD + 2 * A_WIDTH

    @pl.when(ti == 0)
    def _():
        st_ref[...] = jnp.zeros_like(st_ref)
        kp_ref[...] = jnp.zeros_like(kp_ref)
        vp_ref[...] = jnp.zeros_like(vp_ref)

    mod = mod_ref[0]
    h = (x_ref[0] * (1.0 + mod[1:2]) + mod[0:1]).astype(BF16)

    def project(lo, width, dst_ref, dst_lo):
        dst_ref[:, dst_lo:dst_lo + width] = _dot(h, win_ref[:, lo:lo + width])

    project(na, B_WIDTH + 2 * B_KV_WIDTH, zb_ref, 0)
    project(A_KD, A_KD, za_ref, A_KD)
    project(0, A_KD, za_ref, 0)
    project(2 * A_KD, A_WIDTH, za_ref, 2 * A_KD)
    project(2 * A_KD + A_WIDTH, A_WIDTH, za_ref, 2 * A_KD + A_WIDTH)
    swa = _swa_tile(zb_ref, cos_ref, sin_ref, sink_ref, kp_ref, vp_ref, oc_ref,
                    layer_e=layer_e, ti=ti, tq=tq, col0=A_WIDTH)
    next(swa)
    factored_ok = _hgrn_prepare(za_ref, lbraw_ref, k_ref, bl_ref, layer_e=layer_e, tq=tq)
    for _ in swa:
        pass

    def finish(factored):
        y = _dot(oc_ref[:, A_WIDTH:], wout_ref[A_WIDTH:, :])
        _hgrn_chunks(factored, za_ref, ng_ref, st_ref, k_ref, bl_ref, oc_ref, tq=tq)
        y = y + _dot(oc_ref[:, :A_WIDTH], wout_ref[:A_WIDTH, :])
        o_ref[0] = _layer_norm(alpha * x_ref[0] + mod[2:3] * y, lng_ref[...], lnb_ref[...])

    pl.when(factored_ok)(functools.partial(finish, True))
    pl.when(jnp.logical_not(factored_ok))(functools.partial(finish, False))


def _even_layer(x, mod_all, w_in, w_out, lb_raw, norm_g, cos_t, sin_t, sinks, ln_g, ln_b,
                *, layer, alpha, tq):
    bsz, t, d = x.shape
    e = layer // 2
    n_in = w_in.shape[-1]
    w = WINDOW
    assert tq % w == 0 and tq % A_CHUNK == 0
    const2 = lambda b, i: (0, 0)
    return pl.pallas_call(
        functools.partial(_even_kernel, layer_e=e, tq=tq, alpha=alpha),
        grid=(bsz, t // tq),
        in_specs=[
            _row_spec(tq, d),
            _mod_spec(layer, d),
            _layer_spec((d, n_in), e),
            _layer_spec((A_WIDTH + B_WIDTH, d), e),
            pl.BlockSpec(lb_raw.shape, const2),
            _layer_spec((1, A_DV), e),
            _row_spec(tq, LANES),
            _row_spec(tq, LANES),
            pl.BlockSpec(memory_space=pltpu.SMEM),
            _layer_spec((1, d), 2 * layer),
            _layer_spec((1, d), 2 * layer),
        ],
        out_specs=_row_spec(tq, d),
        out_shape=jax.ShapeDtypeStruct((bsz, t, d), F32),
        scratch_shapes=[
            pltpu.VMEM((tq, 2 * A_KD + 2 * A_WIDTH), F32),
            pltpu.VMEM((tq, B_WIDTH + 2 * B_KV_WIDTH), F32),
            pltpu.VMEM((tq, A_WIDTH + B_WIDTH), BF16),
            pltpu.VMEM((A_HEADS, A_DV, A_DK), F32),
            pltpu.VMEM((A_HEADS, tq, A_DK), F32),
            pltpu.VMEM((A_HEADS, tq, A_DK), F32),
            pltpu.VMEM((w, B_KV_WIDTH), F32),
            pltpu.VMEM((w, B_KV_WIDTH), F32),
        ],
        compiler_params=_cparams(2),
        name="even_layer",
    )(x, mod_all, w_in, w_out, lb_raw, norm_g, cos_t, sin_t, sinks, ln_g, ln_b)


def _pool_rows(x_ref, mod, w_ref, scale_ref, lng_ref, lnb_ref, carry_ref, y_ref, out_ref,
               *, t0, alpha):
    n, d = x_ref.shape
    gw = d // len(POOL_WINDOWS)
    t_glob = t0 + lax.broadcasted_iota(jnp.int32, (n, 1), 0)
    for gi, win in enumerate(POOL_WINDOWS):
        cols = slice(gi * gw, (gi + 1) * gw)
        h = x_ref[:, cols] * (1.0 + mod[1:2, cols]) + mod[0:1, cols]
        e = jnp.concatenate([carry_ref[:, cols], h], axis=0)
        carry_ref[:, cols] = h[n - POOL_HALO:]
        s = 1
        while s < win:
            e = e + pltpu.roll(e, s, 0)
            s *= 2
        inv_cnt = 1.0 / jnp.minimum(t_glob + 1, win).astype(F32)
        pooled = e[POOL_HALO:] * inv_cnt - h
        y_ref[:, cols] = _dot(pooled.astype(BF16), w_ref[gi])
        yield
    y = y_ref[...] * scale_ref[...]
    out_ref[...] = _layer_norm(alpha * x_ref[...] + mod[2:3] * y, lng_ref[...], lnb_ref[...])


def _ffn_kernel(*refs, pool, alpha, tm, sub, fc, dff):
    if pool:
        (x_ref, mod_ref, pw_ref, psc_ref, plng_ref, plnb_ref, wup_ref, cw_ref, cb_ref, wdn_ref,
         lng_ref, lnb_ref, o_ref, tail_ref, ubuf_ref, carry_ref, y_ref, x1_ref) = refs
    else:
        (x_ref, mod_ref, wup_ref, cw_ref, cb_ref, wdn_ref, lng_ref, lnb_ref, o_ref,
         tail_ref, ubuf_ref) = refs
    ti = pl.program_id(1)
    nslab = fc // LANES
    nchunk = dff // fc
    nsub = tm // sub

    @pl.when(ti == 0)
    def _():
        tail_ref[...] = jnp.zeros_like(tail_ref)
        if pool:
            carry_ref[...] = jnp.zeros_like(carry_ref)

    mod = mod_ref[0]
    scale = 1.0 + mod[4:5]
    hs = {}

    def mix(si):
        rows = pl.ds(si * sub, sub)
        return _pool_rows(x_ref.at[0, rows], mod, pw_ref, psc_ref, plng_ref, plnb_ref, carry_ref,
                          y_ref, x1_ref.at[rows], t0=ti * tm + si * sub, alpha=alpha)

    def mlp_in(si):
        rows = slice(si * sub, (si + 1) * sub)
        return x1_ref[rows, :] if pool else x_ref[0, rows, :]

    def up(si, ci):
        if si not in hs:
            hs[si] = (mlp_in(si) * scale + mod[3:4]).astype(BF16)
        lo = ci * fc
        return (_dot(hs[si], wup_ref[:, lo:lo + fc]),
                _dot(hs[si], wup_ref[:, dff + lo:dff + lo + fc]))

    mixing = iter(())
    if pool:
        for _ in mix(0):
            pass
    seq = [(si, ci) for si in range(nsub) for ci in range(nchunk)]
    uv = up(*seq[0])
    parts = []
    for idx, (si, ci) in enumerate(seq):
        u, v = uv
        if pool and ci == 0 and si + 1 < nsub:
            mixing = mix(si + 1)
        if idx + 1 < len(seq):
            if seq[idx + 1][0] != si:
                for _ in mixing:
                    pass
            uv = up(*seq[idx + 1])
        next(mixing, None)
        for s in range(nslab):
            slab = ci * nslab + s
            cl = ci * fc + s * LANES
            us = u[:, s * LANES:(s + 1) * LANES]
            buf = ubuf_ref.at[idx % 2, s]
            buf[0:SUBLANES, :] = tail_ref[slab]
            buf[SUBLANES:SUBLANES + sub, :] = us
            u1 = buf[SUBLANES - 1:SUBLANES - 1 + sub, :]
            u2 = buf[SUBLANES - 2:SUBLANES - 2 + sub, :]
            tail_ref[slab] = us[sub - SUBLANES:]
            cw = cw_ref[:, cl:cl + LANES]
            uc = cb_ref[:, cl:cl + LANES] + u2 * cw[0:1] + u1 * cw[1:2] + us * cw[2:3]
            parts.append((_silu(uc) * v[:, s * LANES:(s + 1) * LANES]).astype(BF16))
        if ci == nchunk - 1:
            y = _dot(jnp.concatenate(parts, axis=1), wdn_ref[...])
            parts = []
            o_ref[0, si * sub:(si + 1) * sub, :] = _layer_norm(
                alpha * mlp_in(si) + mod[5:6] * y, lng_ref[...], lnb_ref[...])


def _conv_ffn(x, mod_all, w_up, conv_w, conv_b, w_down, ln_g, ln_b, *, layer, alpha, tm, sub,
              pool=None, fc=256):
    bsz, t, d = x.shape
    dff = w_down.shape[1]
    assert dff % fc == 0 and fc % LANES == 0 and CONV_WIDTH == 3 and tm % sub == 0 and sub >= SUBLANES
    in_specs = [_row_spec(tm, d), _mod_spec(layer, d)]
    args = [x, mod_all]
    scratch = [pltpu.VMEM((dff // LANES, SUBLANES, LANES), F32),
               pltpu.VMEM((2, fc // LANES, sub + SUBLANES, LANES), F32)]
    if pool is not None:
        w_grp, scale = pool
        _, ng, gw, _ = w_grp.shape
        assert POOL_HALO >= max(POOL_WINDOWS) and sub >= POOL_HALO and ng * gw == d
        assert all(w & (w - 1) == 0 and w >= 2 for w in POOL_WINDOWS)
        in_specs += [_layer_spec((ng, gw, gw), layer // 2), _layer_spec((1, d), layer // 2),
                     _layer_spec((1, d), 2 * layer), _layer_spec((1, d), 2 * layer)]
        args += [w_grp, scale, ln_g, ln_b]
        scratch += [pltpu.VMEM((POOL_HALO, d), F32), pltpu.VMEM((sub, d), F32),
                    pltpu.VMEM((tm, d), F32)]
    in_specs += [
        _layer_spec((d, 2 * dff), layer),
        _layer_spec((CONV_WIDTH, dff), layer),
        _layer_spec((1, dff), layer),
        _layer_spec((dff, d), layer),
        _layer_spec((1, d), 2 * layer + 1),
        _layer_spec((1, d), 2 * layer + 1),
    ]
    args += [w_up, conv_w, conv_b, w_down, ln_g, ln_b]
    return pl.pallas_call(
        functools.partial(_ffn_kernel, pool=pool is not None, alpha=alpha, tm=tm, sub=sub, fc=fc,
                          dff=dff),
        grid=(bsz, t // tm),
        in_specs=in_specs,
        out_specs=_row_spec(tm, d),
        out_shape=jax.ShapeDtypeStruct((bsz, t, d), F32),
        scratch_shapes=scratch,
        compiler_params=_cparams(2),
        name="ffn_ln" if pool is None else "pool_ffn_ln",
    )(*args)


def _pick_tile(t, pref):
    tm = min(t, pref)
    assert t % tm == 0
    return tm


def kernel(x, c, positions, ada_w, ada_b, ln_g, ln_b, ab_w_in, ab_w_out, hgrn_lb_raw, hgrn_norm_g,
           attn_sinks, pool_w, pool_scale, ffn_w_up, ffn_conv_w, ffn_conv_b, ffn_w_down):
    bsz, t, d = x.shape
    depth = ada_w.shape[0]
    alpha = (2.0 * depth) ** 0.25
    tq = _pick_tile(t, EVEN_TILE)

    mod_all = _ada_mod(c, ada_w, ada_b).reshape(depth, bsz, 6, d)
    cos_t, sin_t = _rope_tables(positions)
    w_in = ab_w_in.astype(BF16)
    w_out = ab_w_out.astype(BF16)
    w_pool = pool_w.astype(BF16)
    w_up = ffn_w_up.astype(BF16)
    w_down = ffn_w_down.astype(BF16)
    ln_g2 = ln_g.reshape(2 * depth, 1, d)
    ln_b2 = ln_b.reshape(2 * depth, 1, d)
    norm_g = hgrn_norm_g.reshape(-1, 1, A_DV)
    pool_sc = pool_scale.reshape(-1, 1, d)
    conv_b = ffn_conv_b.reshape(depth, 1, -1)

    for l in range(depth):
        if l % 2 == 0:
            x = _even_layer(x, mod_all, w_in, w_out, hgrn_lb_raw, norm_g, cos_t, sin_t, attn_sinks,
                            ln_g2, ln_b2, layer=l, alpha=alpha, tq=tq)
            pool = None
        else:
            pool = (w_pool, pool_sc)
        x = _conv_ffn(x, mod_all, w_up, ffn_conv_w, conv_b, w_down, ln_g2, ln_b2, layer=l,
                      alpha=alpha, tm=_pick_tile(t, FFN_TILE), sub=_pick_tile(t, FFN_SUB), pool=pool)
    return x
```
